```python
import jax
import jax.numpy as jnp
from jax import lax
import numpy as np

D_MODEL = 2048
BATCH = 4
SEQ = 2048
DEPTH = 2

MLSTM_HEADS = 4
MLSTM_V_DIM = D_MODEL // (2 * MLSTM_HEADS)
MLSTM_QK_DIM = MLSTM_V_DIM // 2
MLSTM_CHUNK = 64
CONV_WIDTH = 4
HEAD_DIM = 64
ATTN_HEADS = D_MODEL // (2 * HEAD_DIM)
KV_HEADS = 4
WINDOW = 128
ATTN_BLOCK = WINDOW
D_FF = 4 * D_MODEL
NORM_EPS = 1e-6
N_MOD = 6

MLSTM_QK_W = MLSTM_HEADS * MLSTM_QK_DIM
MLSTM_V_W = MLSTM_HEADS * MLSTM_V_DIM
ATTN_Q_W = ATTN_HEADS * HEAD_DIM
ATTN_KV_W = KV_HEADS * HEAD_DIM
D_MIX = MLSTM_V_W + ATTN_Q_W
IN_SIZES = (MLSTM_QK_W, MLSTM_QK_W, MLSTM_V_W, MLSTM_V_W, MLSTM_HEADS, MLSTM_HEADS, ATTN_Q_W, ATTN_KV_W, ATTN_KV_W)
D_IN = sum(IN_SIZES)

kernel_name = 'hymba_mlstm_swa_sink_alibi_sandwich_adaln'


def rms_norm(x, g):
    xf = x.astype(jnp.float32)
    y = xf * lax.rsqrt(jnp.mean(xf * xf, axis=-1, keepdims=True) + NORM_EPS)
    return (y * g.astype(jnp.float32)).astype(x.dtype)


def causal_depthwise_conv(x, w, b):
    y = lax.conv_general_dilated(
        x, w[:, None, :].astype(x.dtype), window_strides=(1,),
        padding=((CONV_WIDTH - 1, 0),), dimension_numbers=('NWC', 'WIO', 'NWC'),
        feature_group_count=x.shape[-1])
    return y + b.astype(x.dtype)


def mlstm_chunkwise(q, k, v, i_pre, f_pre):
    B, S, H, Dk = q.shape
    Dv = v.shape[-1]
    L = MLSTM_CHUNK
    NC = S // L
    q = q.reshape(B, NC, L, H, Dk) * (Dk ** -0.5)
    k = k.reshape(B, NC, L, H, Dk)
    v = v.reshape(B, NC, L, H, Dv)
    ig = i_pre.reshape(B, NC, L, H)
    b = jnp.cumsum(jax.nn.log_sigmoid(f_pre).reshape(B, NC, L, H), axis=2)
    b_end = b[:, :, -1]

    a = b_end[:, :, None] - b + ig
    m_loc = jnp.max(a, axis=2)
    kw = k * jnp.exp(a - m_loc[:, :, None])[..., None]
    C_loc = jnp.einsum('bclhk,bclhv->bchkv', kw, v)
    n_loc = jnp.sum(kw, axis=2)

    def step(carry, inp):
        C, n, m = carry
        be, ml, Cl, nl = inp
        m_new = jnp.maximum(be + m, ml)
        s_prev = jnp.exp(be + m - m_new)
        s_loc = jnp.exp(ml - m_new)
        C_new = s_prev[..., None, None] * C + s_loc[..., None, None] * Cl
        n_new = s_prev[..., None] * n + s_loc[..., None] * nl
        return (C_new, n_new, m_new), (C, n, m)

    init = (jnp.zeros((B, H, Dk, Dv), q.dtype), jnp.zeros((B, H, Dk), q.dtype), jnp.zeros((B, H), q.dtype))
    xs = tuple(jnp.moveaxis(t, 1, 0) for t in (b_end, m_loc, C_loc, n_loc))
    _, (C_prev, n_prev, m_prev) = lax.scan(step, init, xs)
    C_prev = jnp.moveaxis(C_prev, 0, 1)
    n_prev = jnp.moveaxis(n_prev, 0, 1)
    m_prev = jnp.moveaxis(m_prev, 0, 1)

    causal = jnp.tril(jnp.ones((L, L), dtype=bool))
    Dlog = b[:, :, :, None, :] - b[:, :, None, :, :] + ig[:, :, None, :, :]
    Dlog = jnp.where(causal[None, None, :, :, None], Dlog, -jnp.inf)
    inter = b + m_prev[:, :, None, :]
    m_t = jnp.maximum(inter, jnp.max(Dlog, axis=3))
    Pqk = jnp.exp(Dlog - m_t[:, :, :, None, :]) * jnp.einsum('bcthd,bcshd->bctsh', q, k)
    s_inter = jnp.exp(inter - m_t)
    num = (jnp.einsum('bctsh,bcshv->bcthv', Pqk, v)
           + s_inter[..., None] * jnp.einsum('bcthd,bchdv->bcthv', q, C_prev))
    den = jnp.sum(Pqk, axis=3) + s_inter * jnp.einsum('bcthd,bchd->bcth', q, n_prev)
    h = num / jnp.maximum(jnp.abs(den), jnp.exp(-m_t))[..., None]
    return h.reshape(B, S, H, Dv)


def sliding_window_sink_attention(q, k, v, sinks):
    B, S, Hq, Dh = q.shape
    Hkv = k.shape[2]
    G = Hq // Hkv
    T = ATTN_BLOCK
    NB = S // T
    f32 = jnp.float32
    qb = q.astype(f32).reshape(B, NB, T, Hkv, G, Dh) * (Dh ** -0.5)

    def banded(t):
        tb = t.astype(f32).reshape(B, NB, T, Hkv, Dh)
        prev = jnp.pad(tb, ((0, 0), (1, 0), (0, 0), (0, 0), (0, 0)))[:, :-1]
        return jnp.concatenate([prev, tb], axis=2)

    kb, vb = banded(k), banded(v)
    scores = jnp.einsum('bnqhgd,bnkhd->bnhgqk', qb, kb)
    r = jnp.arange(T)[:, None]
    u = jnp.arange(2 * T)[None, :]
    dist = r + T - u
    key_pos = jnp.arange(NB)[:, None] * T + jnp.arange(2 * T)[None, :] - T
    valid = ((dist >= 0) & (dist < WINDOW))[None] & (key_pos >= 0)[:, None, :]
    slopes = jnp.exp2(-8.0 * (jnp.arange(Hq, dtype=f32) + 1.0) / Hq).reshape(Hkv, G)
    alibi = -slopes[:, :, None, None] * dist.astype(f32)
    scores = jnp.where(valid[None, :, None, None], scores + alibi[None, None], -jnp.inf)
    sink = sinks.astype(f32).reshape(Hkv, G)[None, None, :, :, None, None]
    m = jnp.maximum(jnp.max(scores, axis=-1, keepdims=True), sink)
    p = jnp.exp(scores - m)
    probs = p / (jnp.sum(p, axis=-1, keepdims=True) + jnp.exp(sink - m))
    out = jnp.einsum('bnhgqk,bnkhd->bnqhgd', probs, vb)
    return out.reshape(B, S, Hq, Dh)


def token_mixer(h, w_in, conv_w, conv_b, b_i, b_f, g_mlstm_head, g_attn_out, attn_sinks, w_out):
    B, S, _ = h.shape
    f32 = jnp.float32
    proj = h @ w_in
    offs = [int(o) for o in np.cumsum(IN_SIZES)[:-1]]
    q_m, k_m, v_m, o_m, i_pre, f_pre, q_a, k_a, v_a = jnp.split(proj, offs, axis=-1)

    qk = jax.nn.silu(causal_depthwise_conv(jnp.concatenate([q_m, k_m], axis=-1), conv_w, conv_b))
    q_m, k_m = jnp.split(qk, 2, axis=-1)
    hm = mlstm_chunkwise(
        q_m.reshape(B, S, MLSTM_HEADS, MLSTM_QK_DIM).astype(f32),
        k_m.reshape(B, S, MLSTM_HEADS, MLSTM_QK_DIM).astype(f32),
        v_m.reshape(B, S, MLSTM_HEADS, MLSTM_V_DIM).astype(f32),
        (i_pre + b_i).astype(f32), (f_pre + b_f).astype(f32))
    hm = rms_norm(hm, g_mlstm_head) * jax.nn.sigmoid(o_m.reshape(B, S, MLSTM_HEADS, MLSTM_V_DIM).astype(f32))
    hm = hm.reshape(B, S, MLSTM_V_W).astype(h.dtype)

    ha = sliding_window_sink_attention(
        q_a.reshape(B, S, ATTN_HEADS, HEAD_DIM),
        k_a.reshape(B, S, KV_HEADS, HEAD_DIM),
        v_a.reshape(B, S, KV_HEADS, HEAD_DIM), attn_sinks)
    ha = rms_norm(ha.reshape(B, S, ATTN_Q_W), g_attn_out).astype(h.dtype)

    return jnp.concatenate([hm, ha], axis=-1) @ w_out


def squared_relu_mlp(h, w_up, w_down):
    return jnp.square(jax.nn.relu(h @ w_up)) @ w_down


def setup_inputs(seed: int = 0) -> dict:
    key = jax.random.key(seed)
    ks = jax.random.split(key, 24)
    nrm = jax.random.normal
    L = DEPTH
    def gain(k, shape):
        return 1.0 + 0.05 * nrm(k, shape, jnp.float32)
    return {
        'x': nrm(ks[0], (BATCH, SEQ, D_MODEL), jnp.float32),
        'c': nrm(ks[1], (BATCH, D_MODEL), jnp.float32),
        'w_ada': 0.5 * D_MODEL ** -0.5 * nrm(ks[2], (L, D_MODEL, N_MOD * D_MODEL), jnp.float32),
        'b_ada': 0.02 * nrm(ks[3], (L, N_MOD * D_MODEL), jnp.float32),
        'g_pre_mix': gain(ks[4], (L, D_MODEL)),
        'g_post_mix': gain(ks[5], (L, D_MODEL)),
        'g_pre_mlp': gain(ks[6], (L, D_MODEL)),
        'g_post_mlp': gain(ks[7], (L, D_MODEL)),
        'w_in': D_MODEL ** -0.5 * nrm(ks[8], (L, D_MODEL, D_IN), jnp.float32),
        'conv_w': CONV_WIDTH ** -0.5 * nrm(ks[9], (L, CONV_WIDTH, 2 * MLSTM_QK_W), jnp.float32),
        'conv_b': 0.02 * nrm(ks[10], (L, 2 * MLSTM_QK_W), jnp.float32),
        'b_i': 0.1 * nrm(ks[11], (L, MLSTM_HEADS), jnp.float32),
        'b_f': jnp.linspace(3.0, 6.0, MLSTM_HEADS, dtype=jnp.float32)[None, :] + 0.1 * nrm(ks[12], (L, MLSTM_HEADS), jnp.float32),
        'g_mlstm_head': gain(ks[13], (L, MLSTM_HEADS, MLSTM_V_DIM)),
        'g_attn_out': gain(ks[14], (L, ATTN_Q_W)),
        'attn_sinks': 0.5 * nrm(ks[15], (L, ATTN_HEADS), jnp.float32),
        'w_out': D_MIX ** -0.5 * nrm(ks[16], (L, D_MIX, D_MODEL), jnp.float32),
        'w_up': D_MODEL ** -0.5 * nrm(ks[17], (L, D_MODEL, D_FF), jnp.float32),
        'w_down': D_FF ** -0.5 * nrm(ks[18], (L, D_FF, D_MODEL), jnp.float32),
    }


def reference(x, c, w_ada, b_ada, g_pre_mix, g_post_mix, g_pre_mlp, g_post_mlp, w_in, conv_w, conv_b,
              b_i, b_f, g_mlstm_head, g_attn_out, attn_sinks, w_out, w_up, w_down):
    c_act = jax.nn.silu(c)
    for l in range(DEPTH):
        mod = c_act @ w_ada[l] + b_ada[l]
        shift_a, scale_a, gate_a, shift_m, scale_m, gate_m = [m[:, None, :] for m in jnp.split(mod, N_MOD, axis=-1)]
        h = rms_norm(x, g_pre_mix[l]) * (1.0 + scale_a) + shift_a
        y = token_mixer(h, w_in[l], conv_w[l], conv_b[l], b_i[l], b_f[l], g_mlstm_head[l],
                        g_attn_out[l], attn_sinks[l], w_out[l])
        x = x + gate_a * rms_norm(y, g_post_mix[l])
        h = rms_norm(x, g_pre_mlp[l]) * (1.0 + scale_m) + shift_m
        y = squared_relu_mlp(h, w_up[l], w_down[l])
        x = x + gate_m * rms_norm(y, g_post_mlp[l])
    return x
```

```python
import functools

import jax
import jax.numpy as jnp
from jax import lax
from jax.experimental import pallas as pl
from jax.experimental.pallas import tpu as pltpu

F32 = jnp.float32
BF16 = jnp.bfloat16

NORM_EPS = 1e-6
N_MOD = 6
CONV_WIDTH = 4
MLSTM_HEADS = 4
ATTN_HEADS = 16
KV_HEADS = 4
HEAD_DIM = 64
WINDOW = 128
GATE_LANES = 128
CARRY_ROWS = 8

VMEM_LIMIT = 56 * 1024 * 1024

HIGHEST = lax.Precision.HIGHEST


def _sigmoid(x):
    return 1.0 / (1.0 + jnp.exp(-x))


def _log_sigmoid(x):
    return jnp.minimum(x, 0.0) - jnp.log1p(jnp.exp(-jnp.abs(x)))


def _rms(x):
    return x * lax.rsqrt(jnp.mean(x * x, axis=-1, keepdims=True) + NORM_EPS)


def _params(n_axes):
    return pltpu.CompilerParams(dimension_semantics=("arbitrary",) * n_axes,
                                vmem_limit_bytes=VMEM_LIMIT)


def _resident(shape, index_map):
    return pl.BlockSpec(shape, index_map, pipeline_mode=pl.Buffered(1))


def _ada_kernel(c_ref, w_ref, b_ref, o_ref):
    c = c_ref[...]
    ca = (c * _sigmoid(c)).astype(BF16)
    o_ref[...] = jnp.dot(ca, w_ref[...].astype(BF16), preferred_element_type=F32) + b_ref[...]


def _ada(c_pad, w_ada, b_ada):
    depth, d, n = w_ada.shape
    rows = c_pad.shape[0]
    tn = 1024
    return pl.pallas_call(
        _ada_kernel,
        grid=(depth, n // tn),
        in_specs=[
            pl.BlockSpec((rows, d), lambda l, j: (0, 0)),
            pl.BlockSpec((None, d, tn), lambda l, j: (l, 0, j)),
            pl.BlockSpec((None, 1, tn), lambda l, j: (l, 0, j)),
        ],
        out_specs=pl.BlockSpec((None, rows, tn), lambda l, j: (l, 0, j)),
        out_shape=jax.ShapeDtypeStruct((depth, rows, n), F32),
        compiler_params=_params(2),
        name="ada_mod",
    )(c_pad, w_ada, b_ada.reshape(depth, 1, n))


def _in_kernel(x_ref, shift_ref, scale_ref, g_ref, wqk_ref, wv_ref, wo_ref, watt_ref, wg_ref,
               convw_ref, convb_ref, gbias_ref,
               qk_ref, v_ref, og_ref, att_ref, gc_ref, gr_ref, carry_ref, buf_ref, *, tm, qk_w, q_scale):
    i = pl.program_id(1)
    h = _rms(x_ref[...]) * g_ref[...]
    h = h * (1.0 + scale_ref[...]) + shift_ref[...]
    hb = h.astype(BF16)

    pq = jnp.dot(hb, wqk_ref[...], preferred_element_type=F32)

    @pl.when(i == 0)
    def _():
        carry_ref[...] = jnp.zeros_like(carry_ref)

    buf_ref[0:CARRY_ROWS, :] = carry_ref[...]
    buf_ref[CARRY_ROWS:CARRY_ROWS + tm, :] = pq
    carry_ref[...] = pq[tm - CARRY_ROWS:tm, :]
    cw = convw_ref[...]
    y = cw[CONV_WIDTH - 1:CONV_WIDTH, :] * pq + convb_ref[...]
    for j in range(CONV_WIDTH - 1):
        back = CONV_WIDTH - 1 - j
        y = y + cw[j:j + 1, :] * buf_ref[CARRY_ROWS - back:CARRY_ROWS - back + tm, :]
    y = y * _sigmoid(y)
    qk_ref[:, 0:qk_w] = (y[:, 0:qk_w] * q_scale).astype(BF16)
    qk_ref[:, qk_w:] = y[:, qk_w:].astype(BF16)

    v_ref[...] = jnp.dot(hb, wv_ref[...], preferred_element_type=F32).astype(BF16)
    og_ref[...] = _sigmoid(jnp.dot(hb, wo_ref[...], preferred_element_type=F32)).astype(og_ref.dtype)

    pa = jnp.dot(hb, watt_ref[...], preferred_element_type=F32)
    aq_w = ATTN_HEADS * HEAD_DIM
    att_ref[:, 0:aq_w] = (pa[:, 0:aq_w] * (HEAD_DIM ** -0.5)).astype(BF16)
    att_ref[:, aq_w:] = pa[:, aq_w:].astype(BF16)

    pg = jnp.dot(hb, wg_ref[...], preferred_element_type=F32) + gbias_ref[...]
    lane = lax.broadcasted_iota(jnp.int32, pg.shape, 1)
    gc = jnp.where(lane >= MLSTM_HEADS, _log_sigmoid(pg), pg)
    gc_ref[...] = gc
    gr_ref[...] = gc.T[0:2 * MLSTM_HEADS, :]


def _in_proj(x, mod, l, g_pre, wqk, wv, wo, watt, wg, convw, convb, gbias, *, tm):
    b, s, d = x.shape
    qk_w = wqk.shape[1] // 2
    kern = functools.partial(_in_kernel, tm=tm, qk_w=qk_w,
                             q_scale=(qk_w // MLSTM_HEADS) ** -0.5)

    def mod_spec(k):
        return pl.BlockSpec((None, None, None, 1, d), lambda bi, i: (l, bi, k, 0, 0))

    def w_spec(w):
        return _resident(w.shape, lambda bi, i: (0, 0))

    def out_spec(n):
        return pl.BlockSpec((None, tm, n), lambda bi, i: (bi, i, 0))

    n_att = watt.shape[1]
    return pl.pallas_call(
        kern,
        grid=(b, s // tm),
        in_specs=[
            pl.BlockSpec((None, tm, d), lambda bi, i: (bi, i, 0)),
            mod_spec(0), mod_spec(1),
            pl.BlockSpec((None, 1, d), lambda bi, i: (l, 0, 0)),
            w_spec(wqk), w_spec(wv), w_spec(wo), w_spec(watt), w_spec(wg),
            w_spec(convw), w_spec(convb), w_spec(gbias),
        ],
        out_specs=[
            out_spec(wqk.shape[1]), out_spec(wv.shape[1]), out_spec(wo.shape[1]), out_spec(n_att),
            out_spec(GATE_LANES),
            pl.BlockSpec((None, 2 * MLSTM_HEADS, tm), lambda bi, i: (bi, 0, i)),
        ],
        out_shape=[
            jax.ShapeDtypeStruct((b, s, wqk.shape[1]), BF16),
            jax.ShapeDtypeStruct((b, s, wv.shape[1]), BF16),
            jax.ShapeDtypeStruct((b, s, wo.shape[1]), BF16),
            jax.ShapeDtypeStruct((b, s, n_att), BF16),
            jax.ShapeDtypeStruct((b, s, GATE_LANES), F32),
            jax.ShapeDtypeStruct((b, 2 * MLSTM_HEADS, s), F32),
        ],
        scratch_shapes=[
            pltpu.VMEM((CARRY_ROWS, wqk.shape[1]), F32),
            pltpu.VMEM((CARRY_ROWS + tm, wqk.shape[1]), F32),
        ],
        compiler_params=_params(2),
        name="in_proj",
    )(x, mod, mod, g_pre, wqk, wv, wo, watt, wg, convw, convb, gbias)


def _mlstm_kernel(qk_ref, v_ref, og_ref, gc_ref, gr_ref, gh_ref, out_ref, c_ref, n_ref, m_ref, *, chunk):
    j = pl.program_id(1)
    nh = MLSTM_HEADS
    dk = qk_ref.shape[1] // (2 * nh)
    dv = v_ref.shape[1] // nh

    @pl.when(j == 0)
    def _():
        c_ref[...] = jnp.zeros_like(c_ref)
        n_ref[...] = jnp.zeros_like(n_ref)
        m_ref[...] = jnp.zeros_like(m_ref)

    row = lax.broadcasted_iota(jnp.int32, (chunk, chunk), 0)
    col = lax.broadcasted_iota(jnp.int32, (chunk, chunk), 1)
    causal = col <= row
    tri = causal.astype(F32)
    tri_t = (row <= col).astype(F32)
    gc = gc_ref[...]
    gr = gr_ref[...]
    cum_c = jnp.dot(tri, gc, precision=HIGHEST, preferred_element_type=F32)
    cum_r = jnp.dot(gr, tri_t, precision=HIGHEST, preferred_element_type=F32)

    for h in range(nh):
        ig_c = gc[:, h:h + 1]
        b_c = cum_c[:, nh + h:nh + h + 1]
        ig_r = gr[h:h + 1, :]
        b_r = cum_r[nh + h:nh + h + 1, :]
        b_end = b_c[chunk - 1:chunk, :]
        m_prev = m_ref[h][:, 0:1]
        q = qk_ref[:, h * dk:(h + 1) * dk]
        k = qk_ref[:, (nh + h) * dk:(nh + h + 1) * dk]
        v = v_ref[:, h * dv:(h + 1) * dv]

        dlog = jnp.where(causal, b_c - b_r + ig_r, -jnp.inf)
        inter = b_c + m_prev
        m_t = jnp.maximum(inter, jnp.max(dlog, axis=-1, keepdims=True))
        s_qk = lax.dot_general(q, k, (((1,), (1,)), ((), ())), preferred_element_type=F32)
        p = jnp.exp(dlog - m_t) * s_qk
        s_inter = jnp.exp(inter - m_t)
        num = (jnp.dot(p.astype(BF16), v, preferred_element_type=F32)
               + s_inter * jnp.dot(q, c_ref[h].astype(BF16), preferred_element_type=F32))
        qn = jnp.sum(q.astype(F32) * n_ref[h], axis=-1, keepdims=True)
        den = jnp.sum(p, axis=-1, keepdims=True) + s_inter * qn
        hh = num / jnp.maximum(jnp.abs(den), jnp.exp(-m_t))
        hn = _rms(hh) * gh_ref[:, h * dv:(h + 1) * dv]
        out_ref[:, h * dv:(h + 1) * dv] = (hn * og_ref[:, h * dv:(h + 1) * dv].astype(F32)).astype(out_ref.dtype)

        a = b_end - b_c + ig_c
        m_loc = jnp.max(a, axis=0, keepdims=True)
        kw = k.astype(F32) * jnp.exp(a - m_loc)
        c_loc = jnp.dot(kw.T.astype(BF16), v, preferred_element_type=F32)
        n_loc = jnp.sum(kw, axis=0, keepdims=True)
        m_new = jnp.maximum(b_end + m_prev, m_loc)
        s_prev = jnp.exp(b_end + m_prev - m_new)
        s_loc = jnp.exp(m_loc - m_new)
        c_ref[h] = s_prev * c_ref[h] + s_loc * c_loc
        n_ref[h] = s_prev * n_ref[h] + s_loc * n_loc
        m_ref[h] = jnp.broadcast_to(m_new, m_ref.shape[1:])


def _mlstm(qk, v, og, gc, gr, g_head, *, chunk):
    b, s, _ = qk.shape
    nh = MLSTM_HEADS
    dk = qk.shape[2] // (2 * nh)
    dv = v.shape[2] // nh

    def tok_spec(n):
        return pl.BlockSpec((None, chunk, n), lambda bi, j: (bi, j, 0))

    return pl.pallas_call(
        functools.partial(_mlstm_kernel, chunk=chunk),
        grid=(b, s // chunk),
        in_specs=[
            tok_spec(qk.shape[2]), tok_spec(v.shape[2]), tok_spec(og.shape[2]), tok_spec(GATE_LANES),
            pl.BlockSpec((None, 2 * nh, chunk), lambda bi, j: (bi, 0, j)),
            pl.BlockSpec((1, v.shape[2]), lambda bi, j: (0, 0)),
        ],
        out_specs=tok_spec(v.shape[2]),
        out_shape=jax.ShapeDtypeStruct((b, s, v.shape[2]), BF16),
        scratch_shapes=[
            pltpu.VMEM((nh, dk, dv), F32),
            pltpu.VMEM((nh, 1, dk), F32),
            pltpu.VMEM((nh, 1, 128), F32),
        ],
        compiler_params=_params(2),
        name="mlstm",
    )(qk, v, og, gc, gr, g_head)


def _swa_kernel(sink_ref, q_ref, kvp_ref, kvc_ref, g_ref, out_ref, o_acc, *, slopes):
    n = pl.program_id(1)
    t = q_ref.shape[0]
    kv_w = KV_HEADS * HEAD_DIM
    group = ATTN_HEADS // KV_HEADS
    kv = jnp.concatenate([kvp_ref[...], kvc_ref[...]], axis=0)
    r = lax.broadcasted_iota(jnp.int32, (t, 2 * t), 0)
    u = lax.broadcasted_iota(jnp.int32, (t, 2 * t), 1)
    dist = r + t - u
    key_pos = u + (n - 1) * t
    valid = (dist >= 0) & (dist < WINDOW) & (key_pos >= 0)
    dist_f = dist.astype(F32)
    for g in range(KV_HEADS):
        k = kv[:, g * HEAD_DIM:(g + 1) * HEAD_DIM]
        v = kv[:, kv_w + g * HEAD_DIM:kv_w + (g + 1) * HEAD_DIM]
        for jj in range(group):
            hd = g * group + jj
            q = q_ref[:, hd * HEAD_DIM:(hd + 1) * HEAD_DIM]
            s = lax.dot_general(q, k, (((1,), (1,)), ((), ())), preferred_element_type=F32)
            s = jnp.where(valid, s - slopes[hd] * dist_f, -jnp.inf)
            sink = sink_ref[hd]
            m = jnp.maximum(jnp.max(s, axis=-1, keepdims=True), sink)
            p = jnp.exp(s - m)
            denom = jnp.sum(p, axis=-1, keepdims=True) + jnp.exp(sink - m)
            o = jnp.dot(p.astype(BF16), v, preferred_element_type=F32)
            o_acc[:, hd * HEAD_DIM:(hd + 1) * HEAD_DIM] = o / denom
    out_ref[...] = (_rms(o_acc[...]) * g_ref[...]).astype(out_ref.dtype)


def _swa(att, sinks, g_attn):
    b, s, _ = att.shape
    t = WINDOW
    q_w = ATTN_HEADS * HEAD_DIM
    kv_w2 = 2 * KV_HEADS * HEAD_DIM
    kv_blk = q_w // kv_w2
    slopes = tuple(2.0 ** (-8.0 * (h + 1) / ATTN_HEADS) for h in range(ATTN_HEADS))
    return pl.pallas_call(
        functools.partial(_swa_kernel, slopes=slopes),
        grid=(b, s // t),
        in_specs=[
            pl.BlockSpec(memory_space=pltpu.SMEM),
            pl.BlockSpec((None, t, q_w), lambda bi, n: (bi, n, 0)),
            pl.BlockSpec((None, t, kv_w2), lambda bi, n: (bi, jnp.maximum(n - 1, 0), kv_blk)),
            pl.BlockSpec((None, t, kv_w2), lambda bi, n: (bi, n, kv_blk)),
            pl.BlockSpec((1, q_w), lambda bi, n: (0, 0)),
        ],
        out_specs=pl.BlockSpec((None, t, q_w), lambda bi, n: (bi, n, 0)),
        out_shape=jax.ShapeDtypeStruct((b, s, q_w), BF16),
        scratch_shapes=[pltpu.VMEM((t, q_w), F32)],
        compiler_params=_params(2),
        name="swa",
    )(sinks, att, att, att, g_attn)


def _out_kernel(hm_ref, ha_ref, w_ref, x_ref, gate_ref, g_ref, o_ref):
    km = hm_ref.shape[1]
    y = (jnp.dot(hm_ref[...], w_ref[0:km, :], preferred_element_type=F32)
         + jnp.dot(ha_ref[...], w_ref[km:, :], preferred_element_type=F32))
    o_ref[...] = x_ref[...] + gate_ref[...] * (_rms(y) * g_ref[...])


def _out_proj(hm, ha, w_out, x, mod, l, g_post, *, tm):
    b, s, d = x.shape

    def tok_spec(n):
        return pl.BlockSpec((None, tm, n), lambda bi, i: (bi, i, 0))

    return pl.pallas_call(
        _out_kernel,
        grid=(b, s // tm),
        in_specs=[
            tok_spec(hm.shape[2]), tok_spec(ha.shape[2]),
            _resident(w_out.shape, lambda bi, i: (0, 0)),
            tok_spec(d),
            pl.BlockSpec((None, None, None, 1, d), lambda bi, i: (l, bi, 2, 0, 0)),
            pl.BlockSpec((None, 1, d), lambda bi, i: (l, 0, 0)),
        ],
        out_specs=tok_spec(d),
        out_shape=jax.ShapeDtypeStruct((b, s, d), F32),
        compiler_params=_params(2),
        name="out_proj",
    )(hm, ha, w_out, x, mod, g_post)


def _mlp_kernel(x_ref, shift_ref, scale_ref, gate_ref, gpre_ref, gpost_ref, wup_ref, wdn_ref, o_ref,
                h_ref, acc_ref):
    j = pl.program_id(2)

    @pl.when(j == 0)
    def _():
        h = _rms(x_ref[...]) * gpre_ref[...]
        h_ref[...] = (h * (1.0 + scale_ref[...]) + shift_ref[...]).astype(BF16)
        acc_ref[...] = jnp.zeros_like(acc_ref)

    u = jnp.dot(h_ref[...], wup_ref[...], preferred_element_type=F32)
    a = jnp.square(jnp.maximum(u, 0.0)).astype(BF16)
    acc_ref[...] += jnp.dot(a, wdn_ref[...], preferred_element_type=F32)

    @pl.when(j == pl.num_programs(2) - 1)
    def _():
        o_ref[...] = x_ref[...] + gate_ref[...] * (_rms(acc_ref[...]) * gpost_ref[...])


def _mlp(x, mod, l, g_pre, g_post, w_up, w_dn, *, tm, tf):
    b, s, d = x.shape
    f = w_up.shape[1]

    def mod_spec(k):
        return pl.BlockSpec((None, None, None, 1, d), lambda bi, i, j: (l, bi, k, 0, 0))

    def g_spec():
        return pl.BlockSpec((None, 1, d), lambda bi, i, j: (l, 0, 0))

    return pl.pallas_call(
        _mlp_kernel,
        grid=(b, s // tm, f // tf),
        in_specs=[
            pl.BlockSpec((None, tm, d), lambda bi, i, j: (bi, i, 0)),
            mod_spec(3), mod_spec(4), mod_spec(5), g_spec(), g_spec(),
            pl.BlockSpec((d, tf), lambda bi, i, j: (0, j)),
            pl.BlockSpec((tf, d), lambda bi, i, j: (j, 0)),
        ],
        out_specs=pl.BlockSpec((None, tm, d), lambda bi, i, j: (bi, i, 0)),
        out_shape=jax.ShapeDtypeStruct((b, s, d), F32),
        scratch_shapes=[pltpu.VMEM((tm, d), BF16), pltpu.VMEM((tm, d), F32)],
        compiler_params=_params(3),
        name="mlp",
    )(x, mod, mod, mod, g_pre, g_post, w_up, w_dn)


def kernel(x, c, w_ada, b_ada, g_pre_mix, g_post_mix, g_pre_mlp, g_post_mlp, w_in, conv_w, conv_b,
           b_i, b_f, g_mlstm_head, g_attn_out, attn_sinks, w_out, w_up, w_down):
    b, s, d = x.shape
    depth = w_ada.shape[0]
    nh = MLSTM_HEADS
    v_w = g_mlstm_head.shape[1] * g_mlstm_head.shape[2]
    qk_w2 = conv_w.shape[2]
    aq_w = ATTN_HEADS * HEAD_DIM
    akv_w = KV_HEADS * HEAD_DIM
    o_v = qk_w2
    o_o = o_v + v_w
    o_g = o_o + v_w
    o_a = o_g + 2 * nh
    n_att = aq_w + 2 * akv_w

    c_pad = jnp.pad(c, ((0, 8 - b), (0, 0)))
    mod = _ada(c_pad, w_ada, b_ada).reshape(depth, 8, N_MOD, 1, d)

    def vec(a):
        return a.reshape(depth, 1, a.shape[-1])

    g_pre_mix, g_post_mix, g_pre_mlp, g_post_mlp = map(vec, (g_pre_mix, g_post_mix, g_pre_mlp, g_post_mlp))

    for l in range(depth):
        w = w_in[l]
        wqk = w[:, 0:o_v].astype(BF16)
        wv = w[:, o_v:o_o].astype(BF16)
        wo = w[:, o_o:o_g].astype(BF16)
        wg = jnp.pad(w[:, o_g:o_a], ((0, 0), (0, GATE_LANES - 2 * nh))).astype(BF16)
        watt = w[:, o_a:o_a + n_att].astype(BF16)
        gbias = jnp.pad(jnp.concatenate([b_i[l], b_f[l]]), (0, GATE_LANES - 2 * nh)).reshape(1, GATE_LANES)

        qk, v, og, att, gc, gr = _in_proj(
            x, mod, l, g_pre_mix, wqk, wv, wo, watt, wg, conv_w[l], conv_b[l].reshape(1, qk_w2), gbias, tm=256)
        hm = _mlstm(qk, v, og, gc, gr, g_mlstm_head[l].reshape(1, v_w), chunk=256)
        ha = _swa(att, attn_sinks[l], g_attn_out[l].reshape(1, aq_w))
        x = _out_proj(hm, ha, w_out[l].astype(BF16), x, mod, l, g_post_mix, tm=512)
        x = _mlp(x, mod, l, g_pre_mlp, g_post_mlp, w_up[l].astype(BF16), w_down[l].astype(BF16), tm=512, tf=512)
    return x
```

```python
import functools

import jax
import jax.numpy as jnp
from jax import lax
from jax.experimental import pallas as pl
from jax.experimental.pallas import tpu as pltpu

F32 = jnp.float32
BF16 = jnp.bfloat16

NORM_EPS = 1e-6
N_MOD = 6
CONV_WIDTH = 4
MLSTM_HEADS = 4
ATTN_HEADS = 16
KV_HEADS = 4
HEAD_DIM = 64
WINDOW = 128
GATE_LANES = 128
CARRY_ROWS = 8
NORM_ROWS = 16
NORM_UNROLL = 8

D_MODEL = 2048
MLSTM_V_W = D_MODEL // 2
MLSTM_QK_W = MLSTM_V_W // 2
ATT_Q_W = ATTN_HEADS * HEAD_DIM
ATT_KV_W = KV_HEADS * HEAD_DIM
ATT_W = ATT_Q_W + 2 * ATT_KV_W
SRC_V = 2 * MLSTM_QK_W
SRC_O = SRC_V + MLSTM_V_W
SRC_G = SRC_O + MLSTM_V_W
SRC_ATT = SRC_G + 2 * MLSTM_HEADS
SRC_END = SRC_ATT + ATT_W
PK_ATT = SRC_G
PK_G = PK_ATT + ATT_W
PK_END = PK_G + GATE_LANES

VMEM_LIMIT = 56 * 1024 * 1024

HIGHEST = lax.Precision.HIGHEST


def _sigmoid(x):
    return 1.0 / (1.0 + jnp.exp(-x))


def _log_sigmoid(x):
    return jnp.minimum(x, 0.0) - jnp.log1p(jnp.exp(-jnp.abs(x)))


def _rms(x):
    return x * lax.rsqrt(jnp.mean(x * x, axis=-1, keepdims=True) + NORM_EPS)


def _params(n_axes):
    return pltpu.CompilerParams(dimension_semantics=("arbitrary",) * n_axes,
                                vmem_limit_bytes=VMEM_LIMIT)


def _resident(shape, index_map):
    return pl.BlockSpec(shape, index_map, pipeline_mode=pl.Buffered(1))


def _ada_kernel(c_ref, w_ref, b_ref, o_ref):
    c = c_ref[...]
    ca = (c * _sigmoid(c)).astype(BF16)
    o_ref[...] = jnp.dot(ca, w_ref[...].astype(BF16), preferred_element_type=F32) + b_ref[...]


def _ada(c_pad, w_ada, b_ada):
    depth, d, n = w_ada.shape
    rows = c_pad.shape[0]
    tn = 1024
    return pl.pallas_call(
        _ada_kernel,
        grid=(depth, n // tn),
        in_specs=[
            pl.BlockSpec((rows, d), lambda l, j: (0, 0)),
            pl.BlockSpec((None, d, tn), lambda l, j: (l, 0, j)),
            pl.BlockSpec((None, 1, tn), lambda l, j: (l, 0, j)),
        ],
        out_specs=pl.BlockSpec((None, rows, tn), lambda l, j: (l, 0, j)),
        out_shape=jax.ShapeDtypeStruct((depth, rows, n), F32),
        compiler_params=_params(2),
        name="ada_mod",
    )(c_pad, w_ada, b_ada.reshape(depth, 1, n))


def _norm_modulate_rows(x_ref, g_ref, scale_ref, shift_ref, h_ref, n_rows):
    gain = g_ref[...] * (1.0 + scale_ref[...])
    shift = shift_ref[...]

    def body(r, carry):
        rows = pl.ds(pl.multiple_of(r * NORM_ROWS, NORM_ROWS), NORM_ROWS)
        h_ref[rows, :] = (_rms(x_ref[rows, :]) * gain + shift).astype(BF16)
        return carry

    lax.fori_loop(0, n_rows // NORM_ROWS, body, 0, unroll=NORM_UNROLL)


def _in_kernel(x_ref, shift_ref, scale_ref, g_ref, w_ref, convw_ref, convb_ref, gbias_ref, wup_ref, wdn_ref,
               qk_ref, v_ref, og_ref, att_ref, gc_ref, gr_ref, wupb_ref, wdnb_ref,
               carry_ref, buf_ref, *, tm):
    i = pl.program_id(1)
    wupb_ref[...] = wup_ref[...].astype(BF16)
    wdnb_ref[...] = wdn_ref[...].astype(BF16)

    h = _rms(x_ref[...]) * g_ref[...]
    hb = (h * (1.0 + scale_ref[...]) + shift_ref[...]).astype(BF16)

    pq = jnp.dot(hb, w_ref[:, 0:SRC_V], preferred_element_type=F32)

    @pl.when(i == 0)
    def _():
        carry_ref[...] = jnp.zeros_like(carry_ref)

    buf_ref[0:CARRY_ROWS, :] = carry_ref[...]
    buf_ref[CARRY_ROWS:CARRY_ROWS + tm, :] = pq
    carry_ref[...] = pq[tm - CARRY_ROWS:tm, :]
    cw = convw_ref[...]
    y = cw[CONV_WIDTH - 1:CONV_WIDTH, :] * pq + convb_ref[...]
    for j in range(CONV_WIDTH - 1):
        back = CONV_WIDTH - 1 - j
        y = y + cw[j:j + 1, :] * buf_ref[CARRY_ROWS - back:CARRY_ROWS - back + tm, :]
    y = y * _sigmoid(y)
    q_scale = (MLSTM_QK_W // MLSTM_HEADS) ** -0.5
    qk_ref[:, 0:MLSTM_QK_W] = (y[:, 0:MLSTM_QK_W] * q_scale).astype(BF16)
    qk_ref[:, MLSTM_QK_W:] = y[:, MLSTM_QK_W:].astype(BF16)

    v_ref[...] = jnp.dot(hb, w_ref[:, SRC_V:SRC_O], preferred_element_type=F32).astype(BF16)
    og_ref[...] = _sigmoid(jnp.dot(hb, w_ref[:, SRC_O:SRC_G], preferred_element_type=F32)).astype(og_ref.dtype)

    pa = jnp.dot(hb, w_ref[:, PK_ATT:PK_G], preferred_element_type=F32)
    att_ref[:, 0:ATT_Q_W] = (pa[:, 0:ATT_Q_W] * (HEAD_DIM ** -0.5)).astype(BF16)
    att_ref[:, ATT_Q_W:] = pa[:, ATT_Q_W:].astype(BF16)

    pg = jnp.dot(hb, w_ref[:, PK_G:PK_END], preferred_element_type=F32) + gbias_ref[...]
    lane = lax.broadcasted_iota(jnp.int32, pg.shape, 1)
    gc = jnp.where(lane >= MLSTM_HEADS, _log_sigmoid(pg), pg)
    gc_ref[...] = gc
    gr_ref[...] = gc.T[0:2 * MLSTM_HEADS, :]


def _in_proj(x, mod, l, g_pre, w_packed, convw, convb, gbias, w_up, w_dn, *, tm):
    b, s, d = x.shape
    n_steps = b * (s // tm)
    f = w_up.shape[2]
    up_rows = d // n_steps
    dn_rows = f // n_steps

    def mod_spec(k):
        return pl.BlockSpec((None, None, None, 1, d), lambda bi, i: (l, bi, k, 0, 0))

    def w_spec(w):
        return _resident(w.shape, lambda bi, i: (0, 0))

    def out_spec(n):
        return pl.BlockSpec((None, tm, n), lambda bi, i: (bi, i, 0))

    def step(bi, i):
        return bi * (s // tm) + i

    return pl.pallas_call(
        functools.partial(_in_kernel, tm=tm),
        grid=(b, s // tm),
        in_specs=[
            pl.BlockSpec((None, tm, d), lambda bi, i: (bi, i, 0)),
            mod_spec(0), mod_spec(1),
            pl.BlockSpec((None, 1, d), lambda bi, i: (l, 0, 0)),
            w_spec(w_packed), w_spec(convw), w_spec(convb), w_spec(gbias),
            pl.BlockSpec((None, up_rows, f), lambda bi, i: (l, step(bi, i), 0)),
            pl.BlockSpec((None, dn_rows, d), lambda bi, i: (l, step(bi, i), 0)),
        ],
        out_specs=[
            out_spec(2 * MLSTM_QK_W), out_spec(MLSTM_V_W), out_spec(MLSTM_V_W), out_spec(ATT_W),
            out_spec(GATE_LANES),
            pl.BlockSpec((None, 2 * MLSTM_HEADS, tm), lambda bi, i: (bi, 0, i)),
            pl.BlockSpec((up_rows, f), lambda bi, i: (step(bi, i), 0)),
            pl.BlockSpec((dn_rows, d), lambda bi, i: (step(bi, i), 0)),
        ],
        out_shape=[
            jax.ShapeDtypeStruct((b, s, 2 * MLSTM_QK_W), BF16),
            jax.ShapeDtypeStruct((b, s, MLSTM_V_W), BF16),
            jax.ShapeDtypeStruct((b, s, MLSTM_V_W), BF16),
            jax.ShapeDtypeStruct((b, s, ATT_W), BF16),
            jax.ShapeDtypeStruct((b, s, GATE_LANES), F32),
            jax.ShapeDtypeStruct((b, 2 * MLSTM_HEADS, s), F32),
            jax.ShapeDtypeStruct((d, f), BF16),
            jax.ShapeDtypeStruct((f, d), BF16),
        ],
        scratch_shapes=[
            pltpu.VMEM((CARRY_ROWS, 2 * MLSTM_QK_W), F32),
            pltpu.VMEM((CARRY_ROWS + tm, 2 * MLSTM_QK_W), F32),
        ],
        compiler_params=_params(2),
        name="in_proj",
    )(x, mod, mod, g_pre, w_packed, convw, convb, gbias, w_up, w_dn)


def _mlstm_kernel(qk_ref, v_ref, og_ref, gc_ref, gr_ref, gh_ref, out_ref, c_ref, n_ref, m_ref, *, chunk):
    j = pl.program_id(1)
    nh = MLSTM_HEADS
    dk = qk_ref.shape[1] // (2 * nh)
    dv = v_ref.shape[1] // nh

    @pl.when(j == 0)
    def _():
        c_ref[...] = jnp.zeros_like(c_ref)
        n_ref[...] = jnp.zeros_like(n_ref)
        m_ref[...] = jnp.zeros_like(m_ref)

    row = lax.broadcasted_iota(jnp.int32, (chunk, chunk), 0)
    col = lax.broadcasted_iota(jnp.int32, (chunk, chunk), 1)
    causal = col <= row
    tri = causal.astype(F32)
    tri_t = (row <= col).astype(F32)
    gc = gc_ref[...]
    gr = gr_ref[...]
    cum_c = jnp.dot(tri, gc, precision=HIGHEST, preferred_element_type=F32)
    cum_r = jnp.dot(gr, tri_t, precision=HIGHEST, preferred_element_type=F32)

    for h in range(nh):
        ig_c = gc[:, h:h + 1]
        b_c = cum_c[:, nh + h:nh + h + 1]
        ig_r = gr[h:h + 1, :]
        b_r = cum_r[nh + h:nh + h + 1, :]
        b_end = b_c[chunk - 1:chunk, :]
        m_prev = m_ref[h][:, 0:1]
        q = qk_ref[:, h * dk:(h + 1) * dk]
        k = qk_ref[:, (nh + h) * dk:(nh + h + 1) * dk]
        v = v_ref[:, h * dv:(h + 1) * dv]

        dlog = jnp.where(causal, b_c - b_r + ig_r, -jnp.inf)
        inter = b_c + m_prev
        m_t = jnp.maximum(inter, jnp.max(dlog, axis=-1, keepdims=True))
        s_qk = lax.dot_general(q, k, (((1,), (1,)), ((), ())), preferred_element_type=F32)
        p = jnp.exp(dlog - m_t) * s_qk
        s_inter = jnp.exp(inter - m_t)
        num = (jnp.dot(p.astype(BF16), v, preferred_element_type=F32)
               + s_inter * jnp.dot(q, c_ref[h].astype(BF16), preferred_element_type=F32))
        qn = jnp.sum(q.astype(F32) * n_ref[h], axis=-1, keepdims=True)
        den = jnp.sum(p, axis=-1, keepdims=True) + s_inter * qn
        hh = num / jnp.maximum(jnp.abs(den), jnp.exp(-m_t))
        hn = _rms(hh) * gh_ref[:, h * dv:(h + 1) * dv]
        out_ref[:, h * dv:(h + 1) * dv] = (hn * og_ref[:, h * dv:(h + 1) * dv].astype(F32)).astype(out_ref.dtype)

        a = b_end - b_c + ig_c
        m_loc = jnp.max(a, axis=0, keepdims=True)
        kw = k.astype(F32) * jnp.exp(a - m_loc)
        c_loc = jnp.dot(kw.T.astype(BF16), v, preferred_element_type=F32)
        n_loc = jnp.sum(kw, axis=0, keepdims=True)
        m_new = jnp.maximum(b_end + m_prev, m_loc)
        s_prev = jnp.exp(b_end + m_prev - m_new)
        s_loc = jnp.exp(m_loc - m_new)
        c_ref[h] = s_prev * c_ref[h] + s_loc * c_loc
        n_ref[h] = s_prev * n_ref[h] + s_loc * n_loc
        m_ref[h] = jnp.broadcast_to(m_new, m_ref.shape[1:])


def _mlstm(qk, v, og, gc, gr, g_head, *, chunk):
    b, s, _ = qk.shape
    nh = MLSTM_HEADS
    dk = qk.shape[2] // (2 * nh)
    dv = v.shape[2] // nh

    def tok_spec(n):
        return pl.BlockSpec((None, chunk, n), lambda bi, j: (bi, j, 0))

    return pl.pallas_call(
        functools.partial(_mlstm_kernel, chunk=chunk),
        grid=(b, s // chunk),
        in_specs=[
            tok_spec(qk.shape[2]), tok_spec(v.shape[2]), tok_spec(og.shape[2]), tok_spec(GATE_LANES),
            pl.BlockSpec((None, 2 * nh, chunk), lambda bi, j: (bi, 0, j)),
            pl.BlockSpec((1, v.shape[2]), lambda bi, j: (0, 0)),
        ],
        out_specs=tok_spec(v.shape[2]),
        out_shape=jax.ShapeDtypeStruct((b, s, v.shape[2]), BF16),
        scratch_shapes=[
            pltpu.VMEM((nh, dk, dv), F32),
            pltpu.VMEM((nh, 1, dk), F32),
            pltpu.VMEM((nh, 1, 128), F32),
        ],
        compiler_params=_params(2),
        name="mlstm",
    )(qk, v, og, gc, gr, g_head)


def _swa_kernel(sink_ref, q_ref, kvp_ref, kvc_ref, g_ref, out_ref, o_acc, *, slopes):
    n = pl.program_id(1)
    t = q_ref.shape[0]
    kv_w = KV_HEADS * HEAD_DIM
    group = ATTN_HEADS // KV_HEADS
    kv = jnp.concatenate([kvp_ref[...], kvc_ref[...]], axis=0)
    r = lax.broadcasted_iota(jnp.int32, (t, 2 * t), 0)
    u = lax.broadcasted_iota(jnp.int32, (t, 2 * t), 1)
    dist = r + t - u
    key_pos = u + (n - 1) * t
    valid = (dist >= 0) & (dist < WINDOW) & (key_pos >= 0)
    dist_f = dist.astype(F32)
    for g in range(KV_HEADS):
        k = kv[:, g * HEAD_DIM:(g + 1) * HEAD_DIM]
        v = kv[:, kv_w + g * HEAD_DIM:kv_w + (g + 1) * HEAD_DIM]
        for jj in range(group):
            hd = g * group + jj
            q = q_ref[:, hd * HEAD_DIM:(hd + 1) * HEAD_DIM]
            s = lax.dot_general(q, k, (((1,), (1,)), ((), ())), preferred_element_type=F32)
            s = jnp.where(valid, s - slopes[hd] * dist_f, -jnp.inf)
            sink = sink_ref[hd]
            m = jnp.maximum(jnp.max(s, axis=-1, keepdims=True), sink)
            p = jnp.exp(s - m)
            denom = jnp.sum(p, axis=-1, keepdims=True) + jnp.exp(sink - m)
            o = jnp.dot(p.astype(BF16), v, preferred_element_type=F32)
            o_acc[:, hd * HEAD_DIM:(hd + 1) * HEAD_DIM] = o / denom
    out_ref[...] = (_rms(o_acc[...]) * g_ref[...]).astype(out_ref.dtype)


def _swa(att, sinks, g_attn):
    b, s, _ = att.shape
    t = WINDOW
    q_w = ATTN_HEADS * HEAD_DIM
    kv_w2 = 2 * KV_HEADS * HEAD_DIM
    kv_blk = q_w // kv_w2
    slopes = tuple(2.0 ** (-8.0 * (h + 1) / ATTN_HEADS) for h in range(ATTN_HEADS))
    return pl.pallas_call(
        functools.partial(_swa_kernel, slopes=slopes),
        grid=(b, s // t),
        in_specs=[
            pl.BlockSpec(memory_space=pltpu.SMEM),
            pl.BlockSpec((None, t, q_w), lambda bi, n: (bi, n, 0)),
            pl.BlockSpec((None, t, kv_w2), lambda bi, n: (bi, jnp.maximum(n - 1, 0), kv_blk)),
            pl.BlockSpec((None, t, kv_w2), lambda bi, n: (bi, n, kv_blk)),
            pl.BlockSpec((1, q_w), lambda bi, n: (0, 0)),
        ],
        out_specs=pl.BlockSpec((None, t, q_w), lambda bi, n: (bi, n, 0)),
        out_shape=jax.ShapeDtypeStruct((b, s, q_w), BF16),
        scratch_shapes=[pltpu.VMEM((t, q_w), F32)],
        compiler_params=_params(2),
        name="swa",
    )(sinks, att, att, att, g_attn)


def _out_kernel(hm_ref, ha_ref, w_ref, x_ref, gate_ref, g_ref, o_ref):
    km = hm_ref.shape[1]
    y = (jnp.dot(hm_ref[...], w_ref[0:km, :], preferred_element_type=F32)
         + jnp.dot(ha_ref[...], w_ref[km:, :], preferred_element_type=F32))
    o_ref[...] = x_ref[...] + gate_ref[...] * (_rms(y) * g_ref[...])


def _out_proj(hm, ha, w_out, x, mod, l, g_post, *, tm):
    b, s, d = x.shape

    def tok_spec(n):
        return pl.BlockSpec((None, tm, n), lambda bi, i: (bi, i, 0))

    return pl.pallas_call(
        _out_kernel,
        grid=(b, s // tm),
        in_specs=[
            tok_spec(hm.shape[2]), tok_spec(ha.shape[2]),
            _resident(w_out.shape, lambda bi, i: (0, 0)),
            tok_spec(d),
            pl.BlockSpec((None, None, None, 1, d), lambda bi, i: (l, bi, 2, 0, 0)),
            pl.BlockSpec((None, 1, d), lambda bi, i: (l, 0, 0)),
        ],
        out_specs=tok_spec(d),
        out_shape=jax.ShapeDtypeStruct((b, s, d), F32),
        compiler_params=_params(2),
        name="out_proj",
    )(hm, ha, w_out, x, mod, g_post)


def _pack_w_in_rows(src_ref, dst_ref):
    dst_ref[:, 0:SRC_G] = src_ref[:, 0:SRC_G].astype(BF16)
    dst_ref[:, PK_ATT:PK_G] = src_ref[:, SRC_ATT:SRC_END].astype(BF16)
    tail = src_ref[:, SRC_G:SRC_G + GATE_LANES]
    lane = lax.broadcasted_iota(jnp.int32, tail.shape, 1)
    dst_ref[:, PK_G:PK_END] = jnp.where(lane < 2 * MLSTM_HEADS, tail, 0.0).astype(BF16)


def _mlp_kernel(x_ref, shift_ref, scale_ref, gate_ref, gpre_ref, gpost_ref, wup_ref, wdn_ref, *rest,
                tm, cast_next):
    if cast_next:
        win_ref, wout_ref, o_ref, winb_ref, woutb_ref, h_ref, acc_ref = rest
        _pack_w_in_rows(win_ref, winb_ref)
        woutb_ref[...] = wout_ref[...].astype(BF16)
    else:
        o_ref, h_ref, acc_ref = rest
    j = pl.program_id(2)

    @pl.when(j == 0)
    def _():
        _norm_modulate_rows(x_ref, gpre_ref, scale_ref, shift_ref, h_ref, tm)
        acc_ref[...] = jnp.zeros_like(acc_ref)

    u = jnp.dot(h_ref[...], wup_ref[...], preferred_element_type=F32)
    a = jnp.square(jnp.maximum(u, 0.0)).astype(BF16)
    acc_ref[...] += jnp.dot(a, wdn_ref[...], preferred_element_type=F32)

    @pl.when(j == pl.num_programs(2) - 1)
    def _():
        gain = gate_ref[...] * gpost_ref[...]

        def body(r, carry):
            rows = pl.ds(pl.multiple_of(r * NORM_ROWS, NORM_ROWS), NORM_ROWS)
            o_ref[rows, :] = x_ref[rows, :] + gain * _rms(acc_ref[rows, :])
            return carry

        lax.fori_loop(0, tm // NORM_ROWS, body, 0, unroll=NORM_UNROLL)


def _mlp(x, mod, l, g_pre, g_post, w_up, w_dn, w_in_next, w_out_next, *, tm, tf):
    b, s, d = x.shape
    f = w_up.shape[1]
    grid = (b, s // tm, f // tf)
    cast_next = w_in_next is not None

    def mod_spec(k):
        return pl.BlockSpec((None, None, None, 1, d), lambda bi, i, j: (l, bi, k, 0, 0))

    def g_spec():
        return pl.BlockSpec((None, 1, d), lambda bi, i, j: (l, 0, 0))

    def step(bi, i, j):
        return (bi * grid[1] + i) * grid[2] + j

    tok_spec = pl.BlockSpec((None, tm, d), lambda bi, i, j: (bi, i, 0))
    in_specs = [
        tok_spec, mod_spec(3), mod_spec(4), mod_spec(5), g_spec(), g_spec(),
        pl.BlockSpec((d, tf), lambda bi, i, j: (0, j)),
        pl.BlockSpec((tf, d), lambda bi, i, j: (j, 0)),
    ]
    out_specs = [tok_spec]
    out_shape = [jax.ShapeDtypeStruct((b, s, d), F32)]
    args = [x, mod, mod, mod, g_pre, g_post, w_up, w_dn]
    if cast_next:
        rows = d // (grid[0] * grid[1] * grid[2])
        in_specs += [
            pl.BlockSpec((None, rows, SRC_END), lambda bi, i, j: (l + 1, step(bi, i, j), 0)),
            pl.BlockSpec((None, rows, d), lambda bi, i, j: (l + 1, step(bi, i, j), 0)),
        ]
        out_specs += [
            pl.BlockSpec((rows, PK_END), lambda bi, i, j: (step(bi, i, j), 0)),
            pl.BlockSpec((rows, d), lambda bi, i, j: (step(bi, i, j), 0)),
        ]
        out_shape += [jax.ShapeDtypeStruct((d, PK_END), BF16), jax.ShapeDtypeStruct((d, d), BF16)]
        args += [w_in_next, w_out_next]

    return pl.pallas_call(
        functools.partial(_mlp_kernel, tm=tm, cast_next=cast_next),
        grid=grid,
        in_specs=in_specs,
        out_specs=out_specs,
        out_shape=out_shape,
        scratch_shapes=[pltpu.VMEM((tm, d), BF16), pltpu.VMEM((tm, d), F32)],
        compiler_params=_params(3),
        name="mlp",
    )(*args)


def kernel(x, c, w_ada, b_ada, g_pre_mix, g_post_mix, g_pre_mlp, g_post_mlp, w_in, conv_w, conv_b,
           b_i, b_f, g_mlstm_head, g_attn_out, attn_sinks, w_out, w_up, w_down):
    b, s, d = x.shape
    depth = w_ada.shape[0]
    nh = MLSTM_HEADS
    assert d == D_MODEL and w_in.shape[2] == SRC_END and conv_w.shape[2] == 2 * MLSTM_QK_W

    c_pad = jnp.pad(c, ((0, 8 - b), (0, 0)))
    mod = _ada(c_pad, w_ada, b_ada).reshape(depth, 8, N_MOD, 1, d)

    def vec(a):
        return a.reshape(depth, 1, a.shape[-1])

    g_pre_mix, g_post_mix, g_pre_mlp, g_post_mlp = map(vec, (g_pre_mix, g_post_mix, g_pre_mlp, g_post_mlp))

    w0 = w_in[0]
    gates0 = jnp.pad(w0[:, SRC_G:SRC_ATT], ((0, 0), (0, GATE_LANES - 2 * nh)))
    w_in_b = jnp.concatenate([w0[:, :SRC_G], w0[:, SRC_ATT:], gates0], axis=1).astype(BF16)
    w_out_b = w_out[0].astype(BF16)

    for l in range(depth):
        gbias = jnp.pad(jnp.concatenate([b_i[l], b_f[l]]), (0, GATE_LANES - 2 * nh)).reshape(1, GATE_LANES)
        qk, v, og, att, gc, gr, w_up_b, w_dn_b = _in_proj(
            x, mod, l, g_pre_mix, w_in_b, conv_w[l], conv_b[l].reshape(1, 2 * MLSTM_QK_W), gbias,
            w_up, w_down, tm=256)
        hm = _mlstm(qk, v, og, gc, gr, g_mlstm_head[l].reshape(1, MLSTM_V_W), chunk=256)
        ha = _swa(att, attn_sinks[l], g_attn_out[l].reshape(1, ATT_Q_W))
        x = _out_proj(hm, ha, w_out_b, x, mod, l, g_post_mix, tm=512)
        if l + 1 < depth:
            x, w_in_b, w_out_b = _mlp(x, mod, l, g_pre_mlp, g_post_mlp, w_up_b, w_dn_b, w_in, w_out,
                                      tm=512, tf=1024)
        else:
            x, = _mlp(x, mod, l, g_pre_mlp, g_post_mlp, w_up_b, w_dn_b, None, None, tm=512, tf=1024)
    return x
```

```python
import functools

import jax
import jax.numpy as jnp
from jax import lax
from jax.experimental import pallas as pl
from jax.experimental.pallas import tpu as pltpu

F32 = jnp.float32
BF16 = jnp.bfloat16

NORM_EPS = 1e-6
N_MOD = 6
CONV_WIDTH = 4
MLSTM_HEADS = 4
ATTN_HEADS = 16
KV_HEADS = 4
HEAD_DIM = 64
WINDOW = 128
GATE_LANES = 128
CARRY_ROWS = 8
NORM_ROWS = 16
NORM_UNROLL = 8

D_MODEL = 2048
MLSTM_V_W = D_MODEL // 2
MLSTM_QK_W = MLSTM_V_W // 2
ATT_Q_W = ATTN_HEADS * HEAD_DIM
ATT_KV_W = KV_HEADS * HEAD_DIM
ATT_W = ATT_Q_W + 2 * ATT_KV_W
SRC_V = 2 * MLSTM_QK_W
SRC_O = SRC_V + MLSTM_V_W
SRC_G = SRC_O + MLSTM_V_W
SRC_ATT = SRC_G + 2 * MLSTM_HEADS
SRC_END = SRC_ATT + ATT_W
PK_ATT = SRC_G
PK_G = PK_ATT + ATT_W
PK_END = PK_G + GATE_LANES
PACK_COLS = 128
N_PACK_BLOCKS = PK_END // PACK_COLS

VMEM_LIMIT = 56 * 1024 * 1024

HIGHEST = lax.Precision.HIGHEST


def _sigmoid(x):
    return 1.0 / (1.0 + jnp.exp(-x))


def _log_sigmoid(x):
    return jnp.minimum(x, 0.0) - jnp.log1p(jnp.exp(-jnp.abs(x)))


def _rms(x):
    return x * lax.rsqrt(jnp.mean(x * x, axis=-1, keepdims=True) + NORM_EPS)


def _params(n_axes):
    return pltpu.CompilerParams(dimension_semantics=("arbitrary",) * n_axes,
                                vmem_limit_bytes=VMEM_LIMIT)


def _resident(shape, index_map):
    return pl.BlockSpec(shape, index_map, pipeline_mode=pl.Buffered(1))


def _ada_kernel(c_ref, w_ref, b_ref, o_ref):
    c = c_ref[...]
    ca = (c * _sigmoid(c)).astype(BF16)
    o_ref[...] = jnp.dot(ca, w_ref[...].astype(BF16), preferred_element_type=F32) + b_ref[...]


def _ada(c_pad, w_ada, b_ada):
    depth, d, n = w_ada.shape
    rows = c_pad.shape[0]
    tn = 1024
    return pl.pallas_call(
        _ada_kernel,
        grid=(depth, n // tn),
        in_specs=[
            pl.BlockSpec((rows, d), lambda l, j: (0, 0)),
            pl.BlockSpec((None, d, tn), lambda l, j: (l, 0, j)),
            pl.BlockSpec((None, 1, tn), lambda l, j: (l, 0, j)),
        ],
        out_specs=pl.BlockSpec((None, rows, tn), lambda l, j: (l, 0, j)),
        out_shape=jax.ShapeDtypeStruct((depth, rows, n), F32),
        compiler_params=_params(2),
        name="ada_mod",
    )(c_pad, w_ada, b_ada.reshape(depth, 1, n))


def _norm_modulate_rows(x_ref, g_ref, scale_ref, shift_ref, h_ref, n_rows):
    gain = g_ref[...] * (1.0 + scale_ref[...])
    shift = shift_ref[...]

    def body(r, carry):
        rows = pl.ds(pl.multiple_of(r * NORM_ROWS, NORM_ROWS), NORM_ROWS)
        h_ref[rows, :] = (_rms(x_ref[rows, :]) * gain + shift).astype(BF16)
        return carry

    lax.fori_loop(0, n_rows // NORM_ROWS, body, 0, unroll=NORM_UNROLL)


def _in_kernel(x_ref, shift_ref, scale_ref, g_ref, w_ref, convw_ref, convb_ref, gbias_ref, wup_ref, wdn_ref,
               qk_ref, v_ref, og_ref, att_ref, gc_ref, gr_ref, wupb_ref, wdnb_ref,
               carry_ref, buf_ref, *, tm):
    i = pl.program_id(1)
    wupb_ref[...] = wup_ref[...].astype(BF16)
    wdnb_ref[...] = wdn_ref[...].astype(BF16)

    h = _rms(x_ref[...]) * g_ref[...]
    hb = (h * (1.0 + scale_ref[...]) + shift_ref[...]).astype(BF16)

    pq = jnp.dot(hb, w_ref[:, 0:SRC_V], preferred_element_type=F32)

    @pl.when(i == 0)
    def _():
        carry_ref[...] = jnp.zeros_like(carry_ref)

    buf_ref[0:CARRY_ROWS, :] = carry_ref[...]
    buf_ref[CARRY_ROWS:CARRY_ROWS + tm, :] = pq
    carry_ref[...] = pq[tm - CARRY_ROWS:tm, :]
    cw = convw_ref[...]
    y = cw[CONV_WIDTH - 1:CONV_WIDTH, :] * pq + convb_ref[...]
    for j in range(CONV_WIDTH - 1):
        back = CONV_WIDTH - 1 - j
        y = y + cw[j:j + 1, :] * buf_ref[CARRY_ROWS - back:CARRY_ROWS - back + tm, :]
    y = y * _sigmoid(y)
    q_scale = (MLSTM_QK_W // MLSTM_HEADS) ** -0.5
    qk_ref[:, 0:MLSTM_QK_W] = (y[:, 0:MLSTM_QK_W] * q_scale).astype(BF16)
    qk_ref[:, MLSTM_QK_W:] = y[:, MLSTM_QK_W:].astype(BF16)

    v_ref[...] = jnp.dot(hb, w_ref[:, SRC_V:SRC_O], preferred_element_type=F32).astype(BF16)
    og_ref[...] = _sigmoid(jnp.dot(hb, w_ref[:, SRC_O:SRC_G], preferred_element_type=F32)).astype(og_ref.dtype)

    pa = jnp.dot(hb, w_ref[:, PK_ATT:PK_G], preferred_element_type=F32)
    att_ref[:, 0:ATT_Q_W] = (pa[:, 0:ATT_Q_W] * (HEAD_DIM ** -0.5)).astype(BF16)
    att_ref[:, ATT_Q_W:] = pa[:, ATT_Q_W:].astype(BF16)

    pg = jnp.dot(hb, w_ref[:, PK_G:PK_END], preferred_element_type=F32) + gbias_ref[...]
    lane = lax.broadcasted_iota(jnp.int32, pg.shape, 1)
    gc = jnp.where(lane >= MLSTM_HEADS, _log_sigmoid(pg), pg)
    gc_ref[...] = gc
    gr_ref[...] = gc.T[0:2 * MLSTM_HEADS, :]


def _in_proj(x, mod, l, g_pre, w_packed, convw, convb, gbias, w_up, w_dn, *, tm):
    b, s, d = x.shape
    n_steps = b * (s // tm)
    f = w_up.shape[2]
    up_rows = d // n_steps
    dn_rows = f // n_steps

    def mod_spec(k):
        return pl.BlockSpec((None, None, None, 1, d), lambda bi, i: (l, bi, k, 0, 0))

    def w_spec(w):
        return _resident(w.shape, lambda bi, i: (0, 0))

    def out_spec(n):
        return pl.BlockSpec((None, tm, n), lambda bi, i: (bi, i, 0))

    def step(bi, i):
        return bi * (s // tm) + i

    return pl.pallas_call(
        functools.partial(_in_kernel, tm=tm),
        grid=(b, s // tm),
        in_specs=[
            pl.BlockSpec((None, tm, d), lambda bi, i: (bi, i, 0)),
            mod_spec(0), mod_spec(1),
            pl.BlockSpec((None, 1, d), lambda bi, i: (l, 0, 0)),
            w_spec(w_packed),
            pl.BlockSpec((None,) + convw.shape[1:], lambda bi, i: (l, 0, 0)),
            pl.BlockSpec((None,) + convb.shape[1:], lambda bi, i: (l, 0, 0)),
            pl.BlockSpec((None,) + gbias.shape[1:], lambda bi, i: (l, 0, 0)),
            pl.BlockSpec((None, up_rows, f), lambda bi, i: (l, step(bi, i), 0)),
            pl.BlockSpec((None, dn_rows, d), lambda bi, i: (l, step(bi, i), 0)),
        ],
        out_specs=[
            out_spec(2 * MLSTM_QK_W), out_spec(MLSTM_V_W), out_spec(MLSTM_V_W), out_spec(ATT_W),
            out_spec(GATE_LANES),
            pl.BlockSpec((None, 2 * MLSTM_HEADS, tm), lambda bi, i: (bi, 0, i)),
            pl.BlockSpec((up_rows, f), lambda bi, i: (step(bi, i), 0)),
            pl.BlockSpec((dn_rows, d), lambda bi, i: (step(bi, i), 0)),
        ],
        out_shape=[
            jax.ShapeDtypeStruct((b, s, 2 * MLSTM_QK_W), BF16),
            jax.ShapeDtypeStruct((b, s, MLSTM_V_W), BF16),
            jax.ShapeDtypeStruct((b, s, MLSTM_V_W), BF16),
            jax.ShapeDtypeStruct((b, s, ATT_W), BF16),
            jax.ShapeDtypeStruct((b, s, GATE_LANES), F32),
            jax.ShapeDtypeStruct((b, 2 * MLSTM_HEADS, s), F32),
            jax.ShapeDtypeStruct((d, f), BF16),
            jax.ShapeDtypeStruct((f, d), BF16),
        ],
        scratch_shapes=[
            pltpu.VMEM((CARRY_ROWS, 2 * MLSTM_QK_W), F32),
            pltpu.VMEM((CARRY_ROWS + tm, 2 * MLSTM_QK_W), F32),
        ],
        compiler_params=_params(2),
        name="in_proj",
    )(x, mod, mod, g_pre, w_packed, convw, convb, gbias, w_up, w_dn)


def _mlstm_kernel(qk_ref, v_ref, og_ref, gc_ref, gr_ref, gh_ref, out_ref, c_ref, n_ref, m_ref, *, chunk):
    j = pl.program_id(1)
    nh = MLSTM_HEADS
    dk = qk_ref.shape[1] // (2 * nh)
    dv = v_ref.shape[1] // nh

    @pl.when(j == 0)
    def _():
        c_ref[...] = jnp.zeros_like(c_ref)
        n_ref[...] = jnp.zeros_like(n_ref)
        m_ref[...] = jnp.zeros_like(m_ref)

    row = lax.broadcasted_iota(jnp.int32, (chunk, chunk), 0)
    col = lax.broadcasted_iota(jnp.int32, (chunk, chunk), 1)
    causal = col <= row
    tri = causal.astype(F32)
    tri_t = (row <= col).astype(F32)
    gc = gc_ref[...]
    gr = gr_ref[...]
    cum_c = jnp.dot(tri, gc, precision=HIGHEST, preferred_element_type=F32)
    cum_r = jnp.dot(gr, tri_t, precision=HIGHEST, preferred_element_type=F32)

    for h in range(nh):
        ig_c = gc[:, h:h + 1]
        b_c = cum_c[:, nh + h:nh + h + 1]
        ig_r = gr[h:h + 1, :]
        b_r = cum_r[nh + h:nh + h + 1, :]
        b_end = b_c[chunk - 1:chunk, :]
        m_prev = m_ref[h][:, 0:1]
        q = qk_ref[:, h * dk:(h + 1) * dk]
        k = qk_ref[:, (nh + h) * dk:(nh + h + 1) * dk]
        v = v_ref[:, h * dv:(h + 1) * dv]

        dlog = jnp.where(causal, b_c - b_r + ig_r, -jnp.inf)
        inter = b_c + m_prev
        m_t = jnp.maximum(inter, jnp.max(dlog, axis=-1, keepdims=True))
        s_qk = lax.dot_general(q, k, (((1,), (1,)), ((), ())), preferred_element_type=F32)
        p = jnp.exp(dlog - m_t) * s_qk
        s_inter = jnp.exp(inter - m_t)
        num = (jnp.dot(p.astype(BF16), v, preferred_element_type=F32)
               + s_inter * jnp.dot(q, c_ref[h].astype(BF16), preferred_element_type=F32))
        qn = jnp.sum(q.astype(F32) * n_ref[h], axis=-1, keepdims=True)
        den = jnp.sum(p, axis=-1, keepdims=True) + s_inter * qn
        hh = num / jnp.maximum(jnp.abs(den), jnp.exp(-m_t))
        hn = _rms(hh) * gh_ref[:, h * dv:(h + 1) * dv]
        out_ref[:, h * dv:(h + 1) * dv] = (hn * og_ref[:, h * dv:(h + 1) * dv].astype(F32)).astype(out_ref.dtype)

        a = b_end - b_c + ig_c
        m_loc = jnp.max(a, axis=0, keepdims=True)
        kw = k.astype(F32) * jnp.exp(a - m_loc)
        c_loc = jnp.dot(kw.T.astype(BF16), v, preferred_element_type=F32)
        n_loc = jnp.sum(kw, axis=0, keepdims=True)
        m_new = jnp.maximum(b_end + m_prev, m_loc)
        s_prev = jnp.exp(b_end + m_prev - m_new)
        s_loc = jnp.exp(m_loc - m_new)
        c_ref[h] = s_prev * c_ref[h] + s_loc * c_loc
        n_ref[h] = s_prev * n_ref[h] + s_loc * n_loc
        m_ref[h] = jnp.broadcast_to(m_new, m_ref.shape[1:])


def _mlstm(qk, v, og, gc, gr, g_head, l, *, chunk):
    b, s, _ = qk.shape
    nh = MLSTM_HEADS
    dk = qk.shape[2] // (2 * nh)
    dv = v.shape[2] // nh

    def tok_spec(n):
        return pl.BlockSpec((None, chunk, n), lambda bi, j: (bi, j, 0))

    return pl.pallas_call(
        functools.partial(_mlstm_kernel, chunk=chunk),
        grid=(b, s // chunk),
        in_specs=[
            tok_spec(qk.shape[2]), tok_spec(v.shape[2]), tok_spec(og.shape[2]), tok_spec(GATE_LANES),
            pl.BlockSpec((None, 2 * nh, chunk), lambda bi, j: (bi, 0, j)),
            pl.BlockSpec((None, 1, v.shape[2]), lambda bi, j: (l, 0, 0)),
        ],
        out_specs=tok_spec(v.shape[2]),
        out_shape=jax.ShapeDtypeStruct((b, s, v.shape[2]), BF16),
        scratch_shapes=[
            pltpu.VMEM((nh, dk, dv), F32),
            pltpu.VMEM((nh, 1, dk), F32),
            pltpu.VMEM((nh, 1, 128), F32),
        ],
        compiler_params=_params(2),
        name="mlstm",
    )(qk, v, og, gc, gr, g_head)


def _swa_kernel(sink_ref, q_ref, kvp_ref, kvc_ref, g_ref, out_ref, o_acc, *, slopes, l):
    n = pl.program_id(1)
    t = q_ref.shape[0]
    kv_w = KV_HEADS * HEAD_DIM
    group = ATTN_HEADS // KV_HEADS
    kv = jnp.concatenate([kvp_ref[...], kvc_ref[...]], axis=0)
    r = lax.broadcasted_iota(jnp.int32, (t, 2 * t), 0)
    u = lax.broadcasted_iota(jnp.int32, (t, 2 * t), 1)
    dist = r + t - u
    key_pos = u + (n - 1) * t
    valid = (dist >= 0) & (dist < WINDOW) & (key_pos >= 0)
    dist_f = dist.astype(F32)
    for g in range(KV_HEADS):
        k = kv[:, g * HEAD_DIM:(g + 1) * HEAD_DIM]
        v = kv[:, kv_w + g * HEAD_DIM:kv_w + (g + 1) * HEAD_DIM]
        for jj in range(group):
            hd = g * group + jj
            q = q_ref[:, hd * HEAD_DIM:(hd + 1) * HEAD_DIM]
            s = lax.dot_general(q, k, (((1,), (1,)), ((), ())), preferred_element_type=F32)
            s = jnp.where(valid, s - slopes[hd] * dist_f, -jnp.inf)
            sink = sink_ref[l, hd]
            m = jnp.maximum(jnp.max(s, axis=-1, keepdims=True), sink)
            p = jnp.exp(s - m)
            denom = jnp.sum(p, axis=-1, keepdims=True) + jnp.exp(sink - m)
            o = jnp.dot(p.astype(BF16), v, preferred_element_type=F32)
            o_acc[:, hd * HEAD_DIM:(hd + 1) * HEAD_DIM] = o / denom
    out_ref[...] = (_rms(o_acc[...]) * g_ref[...]).astype(out_ref.dtype)


def _swa(att, sinks, g_attn, l):
    b, s, _ = att.shape
    t = WINDOW
    q_w = ATTN_HEADS * HEAD_DIM
    kv_w2 = 2 * KV_HEADS * HEAD_DIM
    kv_blk = q_w // kv_w2
    slopes = tuple(2.0 ** (-8.0 * (h + 1) / ATTN_HEADS) for h in range(ATTN_HEADS))
    return pl.pallas_call(
        functools.partial(_swa_kernel, slopes=slopes, l=l),
        grid=(b, s // t),
        in_specs=[
            pl.BlockSpec(memory_space=pltpu.SMEM),
            pl.BlockSpec((None, t, q_w), lambda bi, n: (bi, n, 0)),
            pl.BlockSpec((None, t, kv_w2), lambda bi, n: (bi, jnp.maximum(n - 1, 0), kv_blk)),
            pl.BlockSpec((None, t, kv_w2), lambda bi, n: (bi, n, kv_blk)),
            pl.BlockSpec((None, 1, q_w), lambda bi, n: (l, 0, 0)),
        ],
        out_specs=pl.BlockSpec((None, t, q_w), lambda bi, n: (bi, n, 0)),
        out_shape=jax.ShapeDtypeStruct((b, s, q_w), BF16),
        scratch_shapes=[pltpu.VMEM((t, q_w), F32)],
        compiler_params=_params(2),
        name="swa",
    )(sinks, att, att, att, g_attn)


def _out_kernel(hm_ref, ha_ref, w_ref, x_ref, gate_ref, g_ref, o_ref):
    km = hm_ref.shape[1]
    y = (jnp.dot(hm_ref[...], w_ref[0:km, :], preferred_element_type=F32)
         + jnp.dot(ha_ref[...], w_ref[km:, :], preferred_element_type=F32))
    o_ref[...] = x_ref[...] + gate_ref[...] * (_rms(y) * g_ref[...])


def _out_proj(hm, ha, w_out, x, mod, l, g_post, *, tm):
    b, s, d = x.shape

    def tok_spec(n):
        return pl.BlockSpec((None, tm, n), lambda bi, i: (bi, i, 0))

    return pl.pallas_call(
        _out_kernel,
        grid=(b, s // tm),
        in_specs=[
            tok_spec(hm.shape[2]), tok_spec(ha.shape[2]),
            _resident(w_out.shape, lambda bi, i: (0, 0)),
            tok_spec(d),
            pl.BlockSpec((None, None, None, 1, d), lambda bi, i: (l, bi, 2, 0, 0)),
            pl.BlockSpec((None, 1, d), lambda bi, i: (l, 0, 0)),
        ],
        out_specs=tok_spec(d),
        out_shape=jax.ShapeDtypeStruct((b, s, d), F32),
        compiler_params=_params(2),
        name="out_proj",
    )(hm, ha, w_out, x, mod, g_post)


def _pack_src_row(k):
    tile = CARRY_ROWS
    per_blk = PACK_COLS // tile
    shifted = k * per_blk + (SRC_ATT - PK_ATT) // tile
    tiles = jnp.where(k < PK_ATT // PACK_COLS, k * per_blk, jnp.where(k < PK_G // PACK_COLS, shifted, SRC_G // tile))
    return tiles * tile


def _pack_w_in_block(k, src_ref, dst_ref):
    blk = src_ref[0]
    row = lax.broadcasted_iota(jnp.int32, blk.shape, 0)
    keep = jnp.logical_or(k < PK_G // PACK_COLS, row < 2 * MLSTM_HEADS)
    dst_ref[...] = jnp.where(keep, blk, 0.0).T.astype(BF16)


def _mlp_kernel(x_ref, shift_ref, scale_ref, gate_ref, gpre_ref, gpost_ref, wup_ref, wdn_ref, *rest,
                tm, cast_next):
    j = pl.program_id(2)
    if cast_next:
        win_ref, wout_ref, o_ref, winb_ref, woutb_ref, h_ref, acc_ref = rest
        step = (pl.program_id(0) * pl.num_programs(1) + pl.program_id(1)) * pl.num_programs(2) + j

        @pl.when(step < N_PACK_BLOCKS)
        def _():
            _pack_w_in_block(step, win_ref, winb_ref)

        woutb_ref[...] = wout_ref[...].astype(BF16)
    else:
        o_ref, h_ref, acc_ref = rest

    @pl.when(j == 0)
    def _():
        _norm_modulate_rows(x_ref, gpre_ref, scale_ref, shift_ref, h_ref, tm)
        acc_ref[...] = jnp.zeros_like(acc_ref)

    u = jnp.dot(h_ref[...], wup_ref[...], preferred_element_type=F32)
    a = jnp.square(jnp.maximum(u, 0.0)).astype(BF16)
    acc_ref[...] += jnp.dot(a, wdn_ref[...], preferred_element_type=F32)

    @pl.when(j == pl.num_programs(2) - 1)
    def _():
        gain = gate_ref[...] * gpost_ref[...]

        def body(r, carry):
            rows = pl.ds(pl.multiple_of(r * NORM_ROWS, NORM_ROWS), NORM_ROWS)
            o_ref[rows, :] = x_ref[rows, :] + gain * _rms(acc_ref[rows, :])
            return carry

        lax.fori_loop(0, tm // NORM_ROWS, body, 0, unroll=NORM_UNROLL)


def _mlp(x, mod, l, g_pre, g_post, w_up, w_dn, w_in_next, w_out_next, *, tm, tf):
    b, s, d = x.shape
    f = w_up.shape[1]
    grid = (b, s // tm, f // tf)
    cast_next = w_in_next is not None

    def mod_spec(k):
        return pl.BlockSpec((None, None, None, 1, d), lambda bi, i, j: (l, bi, k, 0, 0))

    def g_spec():
        return pl.BlockSpec((None, 1, d), lambda bi, i, j: (l, 0, 0))

    def step(bi, i, j):
        return (bi * grid[1] + i) * grid[2] + j

    tok_spec = pl.BlockSpec((None, tm, d), lambda bi, i, j: (bi, i, 0))
    in_specs = [
        tok_spec, mod_spec(3), mod_spec(4), mod_spec(5), g_spec(), g_spec(),
        pl.BlockSpec((d, tf), lambda bi, i, j: (0, j)),
        pl.BlockSpec((tf, d), lambda bi, i, j: (j, 0)),
    ]
    out_specs = [tok_spec]
    out_shape = [jax.ShapeDtypeStruct((b, s, d), F32)]
    args = [x, mod, mod, mod, g_pre, g_post, w_up, w_dn]
    if cast_next:
        n_steps = grid[0] * grid[1] * grid[2]
        assert n_steps >= N_PACK_BLOCKS
        rows = d // n_steps

        def pack_blk(bi, i, j):
            return jnp.minimum(step(bi, i, j), N_PACK_BLOCKS - 1)

        in_specs += [
            pl.BlockSpec((pl.Element(1), pl.Element(PACK_COLS), pl.Element(d)),
                         lambda bi, i, j: (l + 1, _pack_src_row(pack_blk(bi, i, j)), 0)),
            pl.BlockSpec((None, rows, d), lambda bi, i, j: (l + 1, step(bi, i, j), 0)),
        ]
        out_specs += [
            pl.BlockSpec((d, PACK_COLS), lambda bi, i, j: (0, pack_blk(bi, i, j))),
            pl.BlockSpec((rows, d), lambda bi, i, j: (step(bi, i, j), 0)),
        ]
        out_shape += [jax.ShapeDtypeStruct((d, PK_END), BF16), jax.ShapeDtypeStruct((d, d), BF16)]
        args += [w_in_next, w_out_next]

    return pl.pallas_call(
        functools.partial(_mlp_kernel, tm=tm, cast_next=cast_next),
        grid=grid,
        in_specs=in_specs,
        out_specs=out_specs,
        out_shape=out_shape,
        scratch_shapes=[pltpu.VMEM((tm, d), BF16), pltpu.VMEM((tm, d), F32)],
        compiler_params=_params(3),
        name="mlp",
    )(*args)


def _cast_kernel(win_ref, wout_ref, winb_ref, woutb_ref, *, n_out_blocks):
    t = pl.program_id(0)
    _pack_w_in_block(t, win_ref, winb_ref)

    @pl.when(t < n_out_blocks)
    def _():
        woutb_ref[...] = wout_ref[...].astype(BF16)


def _cast_first_layer(w_in_t, w_out, *, rows):
    d = w_out.shape[1]
    n_out_blocks = d // rows
    assert n_out_blocks <= N_PACK_BLOCKS

    def out_blk(t):
        return jnp.minimum(t, n_out_blocks - 1)

    return pl.pallas_call(
        functools.partial(_cast_kernel, n_out_blocks=n_out_blocks),
        grid=(N_PACK_BLOCKS,),
        in_specs=[
            pl.BlockSpec((pl.Element(1), pl.Element(PACK_COLS), pl.Element(d)),
                         lambda t: (0, _pack_src_row(t), 0)),
            pl.BlockSpec((None, rows, d), lambda t: (0, out_blk(t), 0)),
        ],
        out_specs=[
            pl.BlockSpec((d, PACK_COLS), lambda t: (0, t)),
            pl.BlockSpec((rows, d), lambda t: (out_blk(t), 0)),
        ],
        out_shape=[jax.ShapeDtypeStruct((d, PK_END), BF16), jax.ShapeDtypeStruct((d, d), BF16)],
        compiler_params=_params(1),
        name="cast_first_layer",
    )(w_in_t, w_out)


def kernel(x, c, w_ada, b_ada, g_pre_mix, g_post_mix, g_pre_mlp, g_post_mlp, w_in, conv_w, conv_b,
           b_i, b_f, g_mlstm_head, g_attn_out, attn_sinks, w_out, w_up, w_down):
    b, s, d = x.shape
    depth = w_ada.shape[0]
    nh = MLSTM_HEADS
    assert d == D_MODEL and w_in.shape[2] == SRC_END and conv_w.shape[2] == 2 * MLSTM_QK_W

    c_pad = jnp.pad(c, ((0, 8 - b), (0, 0)))
    mod = _ada(c_pad, w_ada, b_ada).reshape(depth, 8, N_MOD, 1, d)

    def vec(a):
        return a.reshape(depth, 1, a.shape[-1])

    g_pre_mix, g_post_mix, g_pre_mlp, g_post_mlp = map(vec, (g_pre_mix, g_post_mix, g_pre_mlp, g_post_mlp))

    conv_b, g_attn_out = vec(conv_b), vec(g_attn_out)
    g_head = g_mlstm_head.reshape(depth, 1, MLSTM_V_W)
    gbias = jnp.pad(jnp.concatenate([b_i, b_f], axis=1), ((0, 0), (0, GATE_LANES - 2 * nh)))
    gbias = gbias.reshape(depth, 1, GATE_LANES)

    w_in_t = jnp.swapaxes(w_in, 1, 2)
    w_in_b, w_out_b = _cast_first_layer(w_in_t, w_out, rows=64)

    for l in range(depth):
        qk, v, og, att, gc, gr, w_up_b, w_dn_b = _in_proj(
            x, mod, l, g_pre_mix, w_in_b, conv_w, conv_b, gbias, w_up, w_down, tm=256)
        hm = _mlstm(qk, v, og, gc, gr, g_head, l, chunk=256)
        ha = _swa(att, attn_sinks, g_attn_out, l)
        x = _out_proj(hm, ha, w_out_b, x, mod, l, g_post_mix, tm=512)
        if l + 1 < depth:
            x, w_in_b, w_out_b = _mlp(x, mod, l, g_pre_mlp, g_post_mlp, w_up_b, w_dn_b, w_in_t, w_out,
                                      tm=512, tf=1024)
        else:
            x, = _mlp(x, mod, l, g_pre_mlp, g_post_mlp, w_up_b, w_dn_b, None, None, tm=512, tf=1024)
    return x
```

```python
import functools

import jax
import jax.numpy as jnp
from jax import lax
from jax.experimental import pallas as pl
from jax.experimental.pallas import tpu as pltpu

F32 = jnp.float32
BF16 = jnp.bfloat16

NORM_EPS = 1e-6
N_MOD = 6
CONV_WIDTH = 4
MLSTM_HEADS = 4
ATTN_HEADS = 16
KV_HEADS = 4
HEAD_DIM = 64
WINDOW = 128
GATE_LANES = 128
CARRY_ROWS = 8
NORM_ROWS = 16
NORM_UNROLL = 8

D_MODEL = 2048
MLSTM_V_W = D_MODEL // 2
MLSTM_QK_W = MLSTM_V_W // 2
ATT_Q_W = ATTN_HEADS * HEAD_DIM
ATT_KV_W = KV_HEADS * HEAD_DIM
ATT_W = ATT_Q_W + 2 * ATT_KV_W
SRC_V = 2 * MLSTM_QK_W
SRC_O = SRC_V + MLSTM_V_W
SRC_G = SRC_O + MLSTM_V_W
SRC_ATT = SRC_G + 2 * MLSTM_HEADS
SRC_END = SRC_ATT + ATT_W
PK_ATT = SRC_G
PK_G = PK_ATT + ATT_W
PK_END = PK_G + GATE_LANES
PACK_COLS = 128
N_PACK_BLOCKS = PK_END // PACK_COLS

VMEM_LIMIT = 56 * 1024 * 1024

HIGHEST = lax.Precision.HIGHEST


def _sigmoid(x):
    return 1.0 / (1.0 + jnp.exp(-x))


def _log_sigmoid(x):
    return jnp.minimum(x, 0.0) - jnp.log1p(jnp.exp(-jnp.abs(x)))


def _rms(x):
    return x * lax.rsqrt(jnp.mean(x * x, axis=-1, keepdims=True) + NORM_EPS)


def _params(n_axes):
    return pltpu.CompilerParams(dimension_semantics=("arbitrary",) * n_axes,
                                vmem_limit_bytes=VMEM_LIMIT)


def _resident(shape, index_map):
    return pl.BlockSpec(shape, index_map, pipeline_mode=pl.Buffered(1))


def _ada_kernel(c_ref, w_ref, b_ref, o_ref):
    c = c_ref[...]
    ca = (c * _sigmoid(c)).astype(BF16)
    o_ref[...] = jnp.dot(ca, w_ref[...].astype(BF16), preferred_element_type=F32) + b_ref[...]


def _ada(c_pad, w_ada, b_ada):
    depth, d, n = w_ada.shape
    rows = c_pad.shape[0]
    tn = 1024
    return pl.pallas_call(
        _ada_kernel,
        grid=(depth, n // tn),
        in_specs=[
            pl.BlockSpec((rows, d), lambda l, j: (0, 0)),
            pl.BlockSpec((None, d, tn), lambda l, j: (l, 0, j)),
            pl.BlockSpec((None, 1, tn), lambda l, j: (l, 0, j)),
        ],
        out_specs=pl.BlockSpec((None, rows, tn), lambda l, j: (l, 0, j)),
        out_shape=jax.ShapeDtypeStruct((depth, rows, n), F32),
        compiler_params=_params(2),
        name="ada_mod",
    )(c_pad, w_ada, b_ada.reshape(depth, 1, n))


def _norm_modulate_rows(x_ref, g_ref, scale_ref, shift_ref, h_ref, n_rows):
    gain = g_ref[...] * (1.0 + scale_ref[...])
    shift = shift_ref[...]

    def body(r, carry):
        rows = pl.ds(pl.multiple_of(r * NORM_ROWS, NORM_ROWS), NORM_ROWS)
        h_ref[rows, :] = (_rms(x_ref[rows, :]) * gain + shift).astype(BF16)
        return carry

    lax.fori_loop(0, n_rows // NORM_ROWS, body, 0, unroll=NORM_UNROLL)


def _in_kernel(x_ref, shift_ref, scale_ref, g_ref, w_ref, convw_ref, convb_ref, gbias_ref, wup_ref, wdn_ref,
               qk_ref, v_ref, og_ref, att_ref, gc_ref, gr_ref, wupb_ref, wdnb_ref,
               carry_ref, buf_ref, *, tm):
    i = pl.program_id(1)
    wupb_ref[...] = wup_ref[...].astype(BF16)
    wdnb_ref[...] = wdn_ref[...].astype(BF16)

    h = _rms(x_ref[...]) * g_ref[...]
    hb = (h * (1.0 + scale_ref[...]) + shift_ref[...]).astype(BF16)

    pq = jnp.dot(hb, w_ref[:, 0:SRC_V], preferred_element_type=F32)

    @pl.when(i == 0)
    def _():
        carry_ref[...] = jnp.zeros_like(carry_ref)

    buf_ref[0:CARRY_ROWS, :] = carry_ref[...]
    buf_ref[CARRY_ROWS:CARRY_ROWS + tm, :] = pq
    carry_ref[...] = pq[tm - CARRY_ROWS:tm, :]
    cw = convw_ref[...]
    y = cw[CONV_WIDTH - 1:CONV_WIDTH, :] * pq + convb_ref[...]
    for j in range(CONV_WIDTH - 1):
        back = CONV_WIDTH - 1 - j
        y = y + cw[j:j + 1, :] * buf_ref[CARRY_ROWS - back:CARRY_ROWS - back + tm, :]
    y = y * _sigmoid(y)
    q_scale = (MLSTM_QK_W // MLSTM_HEADS) ** -0.5
    qk_ref[:, 0:MLSTM_QK_W] = (y[:, 0:MLSTM_QK_W] * q_scale).astype(BF16)
    qk_ref[:, MLSTM_QK_W:] = y[:, MLSTM_QK_W:].astype(BF16)

    v_ref[...] = jnp.dot(hb, w_ref[:, SRC_V:SRC_O], preferred_element_type=F32).astype(BF16)
    og_ref[...] = _sigmoid(jnp.dot(hb, w_ref[:, SRC_O:SRC_G], preferred_element_type=F32)).astype(og_ref.dtype)

    pa = jnp.dot(hb, w_ref[:, PK_ATT:PK_G], preferred_element_type=F32)
    att_ref[:, 0:ATT_Q_W] = (pa[:, 0:ATT_Q_W] * (HEAD_DIM ** -0.5)).astype(BF16)
    att_ref[:, ATT_Q_W:] = pa[:, ATT_Q_W:].astype(BF16)

    pg = jnp.dot(hb, w_ref[:, PK_G:PK_END], preferred_element_type=F32) + gbias_ref[...]
    lane = lax.broadcasted_iota(jnp.int32, pg.shape, 1)
    gc = jnp.where(lane >= MLSTM_HEADS, _log_sigmoid(pg), pg)
    gc_ref[...] = gc
    gr_ref[...] = gc.T[0:2 * MLSTM_HEADS, :]


def _in_proj(x, mod, l, g_pre, w_packed, convw, convb, gbias, w_up, w_dn, *, tm):
    b, s, d = x.shape
    n_steps = b * (s // tm)
    f = w_up.shape[2]
    up_rows = d // n_steps
    dn_rows = f // n_steps

    def mod_spec(k):
        return pl.BlockSpec((None, None, None, 1, d), lambda bi, i: (l, bi, k, 0, 0))

    def w_spec(w):
        return _resident(w.shape, lambda bi, i: (0, 0))

    def out_spec(n):
        return pl.BlockSpec((None, tm, n), lambda bi, i: (bi, i, 0))

    def step(bi, i):
        return bi * (s // tm) + i

    return pl.pallas_call(
        functools.partial(_in_kernel, tm=tm),
        grid=(b, s // tm),
        in_specs=[
            pl.BlockSpec((None, tm, d), lambda bi, i: (bi, i, 0)),
            mod_spec(0), mod_spec(1),
            pl.BlockSpec((None, 1, d), lambda bi, i: (l, 0, 0)),
            w_spec(w_packed),
            pl.BlockSpec((None,) + convw.shape[1:], lambda bi, i: (l, 0, 0)),
            pl.BlockSpec((None,) + convb.shape[1:], lambda bi, i: (l, 0, 0)),
            pl.BlockSpec((None,) + gbias.shape[1:], lambda bi, i: (l, 0, 0)),
            pl.BlockSpec((None, up_rows, f), lambda bi, i: (l, step(bi, i), 0)),
            pl.BlockSpec((None, dn_rows, d), lambda bi, i: (l, step(bi, i), 0)),
        ],
        out_specs=[
            out_spec(2 * MLSTM_QK_W), out_spec(MLSTM_V_W), out_spec(MLSTM_V_W), out_spec(ATT_W),
            out_spec(GATE_LANES),
            pl.BlockSpec((None, 2 * MLSTM_HEADS, tm), lambda bi, i: (bi, 0, i)),
            pl.BlockSpec((up_rows, f), lambda bi, i: (step(bi, i), 0)),
            pl.BlockSpec((dn_rows, d), lambda bi, i: (step(bi, i), 0)),
        ],
        out_shape=[
            jax.ShapeDtypeStruct((b, s, 2 * MLSTM_QK_W), BF16),
            jax.ShapeDtypeStruct((b, s, MLSTM_V_W), BF16),
            jax.ShapeDtypeStruct((b, s, MLSTM_V_W), BF16),
            jax.ShapeDtypeStruct((b, s, ATT_W), BF16),
            jax.ShapeDtypeStruct((b, s, GATE_LANES), F32),
            jax.ShapeDtypeStruct((b, 2 * MLSTM_HEADS, s), F32),
            jax.ShapeDtypeStruct((d, f), BF16),
            jax.ShapeDtypeStruct((f, d), BF16),
        ],
        scratch_shapes=[
            pltpu.VMEM((CARRY_ROWS, 2 * MLSTM_QK_W), F32),
            pltpu.VMEM((CARRY_ROWS + tm, 2 * MLSTM_QK_W), F32),
        ],
        compiler_params=_params(2),
        name="in_proj",
    )(x, mod, mod, g_pre, w_packed, convw, convb, gbias, w_up, w_dn)


def _mlstm_kernel(qk_ref, v_ref, og_ref, gc_ref, gr_ref, gh_ref, out_ref, c_ref, n_ref, m_ref, *, chunk):
    j = pl.program_id(1)
    nh = MLSTM_HEADS
    dk = qk_ref.shape[1] // (2 * nh)
    dv = v_ref.shape[1] // nh

    @pl.when(j == 0)
    def _():
        c_ref[...] = jnp.zeros_like(c_ref)
        n_ref[...] = jnp.zeros_like(n_ref)
        m_ref[...] = jnp.zeros_like(m_ref)

    row = lax.broadcasted_iota(jnp.int32, (chunk, chunk), 0)
    col = lax.broadcasted_iota(jnp.int32, (chunk, chunk), 1)
    causal = col <= row
    tri = causal.astype(F32)
    tri_t = (row <= col).astype(F32)
    gc = gc_ref[...]
    gr = gr_ref[...]
    cum_c = jnp.dot(tri, gc, precision=HIGHEST, preferred_element_type=F32)
    cum_r = jnp.dot(gr, tri_t, precision=HIGHEST, preferred_element_type=F32)

    for h in range(nh):
        ig_c = gc[:, h:h + 1]
        b_c = cum_c[:, nh + h:nh + h + 1]
        ig_r = gr[h:h + 1, :]
        b_r = cum_r[nh + h:nh + h + 1, :]
        b_end = b_c[chunk - 1:chunk, :]
        m_prev = m_ref[h][:, 0:1]
        q = qk_ref[:, h * dk:(h + 1) * dk]
        k = qk_ref[:, (nh + h) * dk:(nh + h + 1) * dk]
        v = v_ref[:, h * dv:(h + 1) * dv]

        dlog = jnp.where(causal, b_c - b_r + ig_r, -jnp.inf)
        inter = b_c + m_prev
        m_t = jnp.maximum(inter, jnp.max(dlog, axis=-1, keepdims=True))
        s_qk = lax.dot_general(q, k, (((1,), (1,)), ((), ())), preferred_element_type=F32)
        p = jnp.exp(dlog - m_t) * s_qk
        s_inter = jnp.exp(inter - m_t)
        num = (jnp.dot(p.astype(BF16), v, preferred_element_type=F32)
               + s_inter * jnp.dot(q, c_ref[h].astype(BF16), preferred_element_type=F32))
        qn = jnp.sum(q.astype(F32) * n_ref[h], axis=-1, keepdims=True)
        den = jnp.sum(p, axis=-1, keepdims=True) + s_inter * qn
        hh = num / jnp.maximum(jnp.abs(den), jnp.exp(-m_t))
        hn = _rms(hh) * gh_ref[:, h * dv:(h + 1) * dv]
        out_ref[:, h * dv:(h + 1) * dv] = (hn * og_ref[:, h * dv:(h + 1) * dv].astype(F32)).astype(out_ref.dtype)

        a = b_end - b_c + ig_c
        m_loc = jnp.max(a, axis=0, keepdims=True)
        kw = k.astype(F32) * jnp.exp(a - m_loc)
        c_loc = jnp.dot(kw.T.astype(BF16), v, preferred_element_type=F32)
        n_loc = jnp.sum(kw, axis=0, keepdims=True)
        m_new = jnp.maximum(b_end + m_prev, m_loc)
        s_prev = jnp.exp(b_end + m_prev - m_new)
        s_loc = jnp.exp(m_loc - m_new)
        c_ref[h] = s_prev * c_ref[h] + s_loc * c_loc
        n_ref[h] = s_prev * n_ref[h] + s_loc * n_loc
        m_ref[h] = jnp.broadcast_to(m_new, m_ref.shape[1:])


def _mlstm(qk, v, og, gc, gr, g_head, l, *, chunk):
    b, s, _ = qk.shape
    nh = MLSTM_HEADS
    dk = qk.shape[2] // (2 * nh)
    dv = v.shape[2] // nh

    def tok_spec(n):
        return pl.BlockSpec((None, chunk, n), lambda bi, j: (bi, j, 0))

    return pl.pallas_call(
        functools.partial(_mlstm_kernel, chunk=chunk),
        grid=(b, s // chunk),
        in_specs=[
            tok_spec(qk.shape[2]), tok_spec(v.shape[2]), tok_spec(og.shape[2]), tok_spec(GATE_LANES),
            pl.BlockSpec((None, 2 * nh, chunk), lambda bi, j: (bi, 0, j)),
            pl.BlockSpec((None, 1, v.shape[2]), lambda bi, j: (l, 0, 0)),
        ],
        out_specs=tok_spec(v.shape[2]),
        out_shape=jax.ShapeDtypeStruct((b, s, v.shape[2]), BF16),
        scratch_shapes=[
            pltpu.VMEM((nh, dk, dv), F32),
            pltpu.VMEM((nh, 1, dk), F32),
            pltpu.VMEM((nh, 1, 128), F32),
        ],
        compiler_params=_params(2),
        name="mlstm",
    )(qk, v, og, gc, gr, g_head)


def _swa_kernel(sink_ref, q_ref, kvp_ref, kvc_ref, g_ref, out_ref, bias_ref, *, slopes, l):
    first = jnp.logical_and(pl.program_id(0) == 0, pl.program_id(1) == 0)
    n = pl.program_id(1)
    t = kvp_ref.shape[0]
    n_sub = q_ref.shape[0] // t
    kv_w = KV_HEADS * HEAD_DIM
    group = ATTN_HEADS // KV_HEADS
    assert t & (t - 1) == 0
    log2_t = t.bit_length() - 1
    lane_head = lax.broadcasted_iota(jnp.int32, (1, group * t), 1) >> log2_t

    def per_head(vals):
        row = jnp.full((1, group * t), vals[group - 1], F32)
        for jj in range(group - 2, -1, -1):
            row = jnp.where(lane_head <= jj, vals[jj], row)
        return row

    @pl.when(first)
    def _():
        u = lax.broadcasted_iota(jnp.int32, (2 * t, group * t), 0)
        r = lax.broadcasted_iota(jnp.int32, (2 * t, group * t), 1) & (t - 1)
        dist = r + t - u
        valid = (dist >= 0) & (dist < WINDOW)
        dist_f = dist.astype(F32)
        for g in range(KV_HEADS):
            bias = jnp.where(valid, -per_head(slopes[g * group:(g + 1) * group]) * dist_f, -jnp.inf)
            bias_ref[0, g] = bias
            bias_ref[1, g] = jnp.where(u >= t, bias, -jnp.inf)

    kv = jnp.concatenate([kvp_ref[...], kvc_ref[...]], axis=0)
    v_t = kv[:, kv_w:].astype(F32).T.astype(BF16)
    scores = []
    for sub in range(n_sub):
        for g in range(KV_HEADS):
            k = kv[sub * t:(sub + 2) * t, g * HEAD_DIM:(g + 1) * HEAD_DIM]
            qs = jnp.concatenate(
                [q_ref[sub * t:(sub + 1) * t, (g * group + jj) * HEAD_DIM:(g * group + jj + 1) * HEAD_DIM]
                 for jj in range(group)], axis=0)
            scores.append(lax.dot_general(k, qs, (((1,), (1,)), ((), ())), preferred_element_type=F32))
    for sub in range(n_sub):
        slot = jnp.where(n == 0, 1, 0) if sub == 0 else 0
        heads_t = []
        for g in range(KV_HEADS):
            s = scores[sub * KV_HEADS + g]
            probs, inv_denoms = [], []
            for jj in range(group):
                cols = slice(jj * t, (jj + 1) * t)
                sj = s[:, cols] + bias_ref[slot, g, :, cols]
                sink = sink_ref[l, g * group + jj]
                m = jnp.maximum(jnp.max(sj, axis=0, keepdims=True), sink)
                p = jnp.exp(sj - m)
                inv_denoms.append(1.0 / (jnp.sum(p, axis=0, keepdims=True) + jnp.exp(sink - m)))
                probs.append(p.astype(BF16))
            o_t = jnp.dot(v_t[g * HEAD_DIM:(g + 1) * HEAD_DIM, sub * t:(sub + 2) * t],
                          jnp.concatenate(probs, axis=1), preferred_element_type=F32)
            heads_t += [o_t[:, jj * t:(jj + 1) * t] * inv_denoms[jj] for jj in range(group)]
        o = jnp.concatenate(heads_t, axis=0).T
        out_ref[sub * t:(sub + 1) * t, :] = (_rms(o) * g_ref[...]).astype(out_ref.dtype)


def _swa(att, sinks, g_attn, l, *, n_sub):
    b, s, _ = att.shape
    t = WINDOW
    tq = n_sub * t
    q_w = ATTN_HEADS * HEAD_DIM
    kv_w2 = 2 * KV_HEADS * HEAD_DIM
    kv_blk = q_w // kv_w2
    slopes = tuple(2.0 ** (-8.0 * (h + 1) / ATTN_HEADS) for h in range(ATTN_HEADS))
    return pl.pallas_call(
        functools.partial(_swa_kernel, slopes=slopes, l=l),
        grid=(b, s // tq),
        in_specs=[
            pl.BlockSpec(memory_space=pltpu.SMEM),
            pl.BlockSpec((None, tq, q_w), lambda bi, n: (bi, n, 0)),
            pl.BlockSpec((None, t, kv_w2), lambda bi, n: (bi, jnp.maximum(n * n_sub - 1, 0), kv_blk)),
            pl.BlockSpec((None, tq, kv_w2), lambda bi, n: (bi, n, kv_blk)),
            pl.BlockSpec((None, 1, q_w), lambda bi, n: (l, 0, 0)),
        ],
        out_specs=pl.BlockSpec((None, tq, q_w), lambda bi, n: (bi, n, 0)),
        out_shape=jax.ShapeDtypeStruct((b, s, q_w), BF16),
        scratch_shapes=[
            pltpu.VMEM((2, KV_HEADS, 2 * t, (ATTN_HEADS // KV_HEADS) * t), F32),
        ],
        compiler_params=_params(2),
        name="swa",
    )(sinks, att, att, att, g_attn)


def _out_kernel(hm_ref, ha_ref, w_ref, x_ref, gate_ref, g_ref, o_ref):
    km = hm_ref.shape[1]
    y = (jnp.dot(hm_ref[...], w_ref[0:km, :], preferred_element_type=F32)
         + jnp.dot(ha_ref[...], w_ref[km:, :], preferred_element_type=F32))
    o_ref[...] = x_ref[...] + gate_ref[...] * (_rms(y) * g_ref[...])


def _out_proj(hm, ha, w_out, x, mod, l, g_post, *, tm):
    b, s, d = x.shape

    def tok_spec(n):
        return pl.BlockSpec((None, tm, n), lambda bi, i: (bi, i, 0))

    return pl.pallas_call(
        _out_kernel,
        grid=(b, s // tm),
        in_specs=[
            tok_spec(hm.shape[2]), tok_spec(ha.shape[2]),
            _resident(w_out.shape, lambda bi, i: (0, 0)),
            tok_spec(d),
            pl.BlockSpec((None, None, None, 1, d), lambda bi, i: (l, bi, 2, 0, 0)),
            pl.BlockSpec((None, 1, d), lambda bi, i: (l, 0, 0)),
        ],
        out_specs=tok_spec(d),
        out_shape=jax.ShapeDtypeStruct((b, s, d), F32),
        compiler_params=_params(2),
        name="out_proj",
    )(hm, ha, w_out, x, mod, g_post)


def _pack_src_row(k):
    tile = CARRY_ROWS
    per_blk = PACK_COLS // tile
    shifted = k * per_blk + (SRC_ATT - PK_ATT) // tile
    tiles = jnp.where(k < PK_ATT // PACK_COLS, k * per_blk, jnp.where(k < PK_G // PACK_COLS, shifted, SRC_G // tile))
    return tiles * tile


def _pack_w_in_block(k, src_ref, dst_ref):
    blk = src_ref[0]
    row = lax.broadcasted_iota(jnp.int32, blk.shape, 0)
    keep = jnp.logical_or(k < PK_G // PACK_COLS, row < 2 * MLSTM_HEADS)
    dst_ref[...] = jnp.where(keep, blk, 0.0).T.astype(BF16)


def _mlp_kernel(x_ref, shift_ref, scale_ref, gate_ref, gpre_ref, gpost_ref, wup_ref, wdn_ref, *rest,
                tm, cast_next):
    j = pl.program_id(2)
    if cast_next:
        win_ref, wout_ref, o_ref, winb_ref, woutb_ref, h_ref, acc_ref = rest
        step = (pl.program_id(0) * pl.num_programs(1) + pl.program_id(1)) * pl.num_programs(2) + j

        @pl.when(step < N_PACK_BLOCKS)
        def _():
            _pack_w_in_block(step, win_ref, winb_ref)

        woutb_ref[...] = wout_ref[...].astype(BF16)
    else:
        o_ref, h_ref, acc_ref = rest

    @pl.when(j == 0)
    def _():
        _norm_modulate_rows(x_ref, gpre_ref, scale_ref, shift_ref, h_ref, tm)
        acc_ref[...] = jnp.zeros_like(acc_ref)

    u = jnp.dot(h_ref[...], wup_ref[...], preferred_element_type=F32)
    a = jnp.square(jnp.maximum(u, 0.0)).astype(BF16)
    acc_ref[...] += jnp.dot(a, wdn_ref[...], preferred_element_type=F32)

    @pl.when(j == pl.num_programs(2) - 1)
    def _():
        gain = gate_ref[...] * gpost_ref[...]

        def body(r, carry):
            rows = pl.ds(pl.multiple_of(r * NORM_ROWS, NORM_ROWS), NORM_ROWS)
            o_ref[rows, :] = x_ref[rows, :] + gain * _rms(acc_ref[rows, :])
            return carry

        lax.fori_loop(0, tm // NORM_ROWS, body, 0, unroll=NORM_UNROLL)


def _mlp(x, mod, l, g_pre, g_post, w_up, w_dn, w_in_next, w_out_next, *, tm, tf):
    b, s, d = x.shape
    f = w_up.shape[1]
    grid = (b, s // tm, f // tf)
    cast_next = w_in_next is not None

    def mod_spec(k):
        return pl.BlockSpec((None, None, None, 1, d), lambda bi, i, j: (l, bi, k, 0, 0))

    def g_spec():
        return pl.BlockSpec((None, 1, d), lambda bi, i, j: (l, 0, 0))

    def step(bi, i, j):
        return (bi * grid[1] + i) * grid[2] + j

    tok_spec = pl.BlockSpec((None, tm, d), lambda bi, i, j: (bi, i, 0))
    in_specs = [
        tok_spec, mod_spec(3), mod_spec(4), mod_spec(5), g_spec(), g_spec(),
        pl.BlockSpec((d, tf), lambda bi, i, j: (0, j)),
        pl.BlockSpec((tf, d), lambda bi, i, j: (j, 0)),
    ]
    out_specs = [tok_spec]
    out_shape = [jax.ShapeDtypeStruct((b, s, d), F32)]
    args = [x, mod, mod, mod, g_pre, g_post, w_up, w_dn]
    if cast_next:
        n_steps = grid[0] * grid[1] * grid[2]
        assert n_steps >= N_PACK_BLOCKS
        rows = d // n_steps

        def pack_blk(bi, i, j):
            return jnp.minimum(step(bi, i, j), N_PACK_BLOCKS - 1)

        in_specs += [
            pl.BlockSpec((pl.Element(1), pl.Element(PACK_COLS), pl.Element(d)),
                         lambda bi, i, j: (l + 1, _pack_src_row(pack_blk(bi, i, j)), 0)),
            pl.BlockSpec((None, rows, d), lambda bi, i, j: (l + 1, step(bi, i, j), 0)),
        ]
        out_specs += [
            pl.BlockSpec((d, PACK_COLS), lambda bi, i, j: (0, pack_blk(bi, i, j))),
            pl.BlockSpec((rows, d), lambda bi, i, j: (step(bi, i, j), 0)),
        ]
        out_shape += [jax.ShapeDtypeStruct((d, PK_END), BF16), jax.ShapeDtypeStruct((d, d), BF16)]
        args += [w_in_next, w_out_next]

    return pl.pallas_call(
        functools.partial(_mlp_kernel, tm=tm, cast_next=cast_next),
        grid=grid,
        in_specs=in_specs,
        out_specs=out_specs,
        out_shape=out_shape,
        scratch_shapes=[pltpu.VMEM((tm, d), BF16), pltpu.VMEM((tm, d), F32)],
        compiler_params=_params(3),
        name="mlp",
    )(*args)


def _cast_kernel(win_ref, wout_ref, winb_ref, woutb_ref, *, n_out_blocks):
    t = pl.program_id(0)
    _pack_w_in_block(t, win_ref, winb_ref)

    @pl.when(t < n_out_blocks)
    def _():
        woutb_ref[...] = wout_ref[...].astype(BF16)


def _cast_first_layer(w_in_t, w_out, *, rows):
    d = w_out.shape[1]
    n_out_blocks = d // rows
    assert n_out_blocks <= N_PACK_BLOCKS

    def out_blk(t):
        return jnp.minimum(t, n_out_blocks - 1)

    return pl.pallas_call(
        functools.partial(_cast_kernel, n_out_blocks=n_out_blocks),
        grid=(N_PACK_BLOCKS,),
        in_specs=[
            pl.BlockSpec((pl.Element(1), pl.Element(PACK_COLS), pl.Element(d)),
                         lambda t: (0, _pack_src_row(t), 0)),
            pl.BlockSpec((None, rows, d), lambda t: (0, out_blk(t), 0)),
        ],
        out_specs=[
            pl.BlockSpec((d, PACK_COLS), lambda t: (0, t)),
            pl.BlockSpec((rows, d), lambda t: (out_blk(t), 0)),
        ],
        out_shape=[jax.ShapeDtypeStruct((d, PK_END), BF16), jax.ShapeDtypeStruct((d, d), BF16)],
        compiler_params=_params(1),
        name="cast_first_layer",
    )(w_in_t, w_out)


def kernel(x, c, w_ada, b_ada, g_pre_mix, g_post_mix, g_pre_mlp, g_post_mlp, w_in, conv_w, conv_b,
           b_i, b_f, g_mlstm_head, g_attn_out, attn_sinks, w_out, w_up, w_down):
    b, s, d = x.shape
    depth = w_ada.shape[0]
    nh = MLSTM_HEADS
    assert d == D_MODEL and w_in.shape[2] == SRC_END and conv_w.shape[2] == 2 * MLSTM_QK_W

    c_pad = jnp.pad(c, ((0, 8 - b), (0, 0)))
    mod = _ada(c_pad, w_ada, b_ada).reshape(depth, 8, N_MOD, 1, d)

    def vec(a):
        return a.reshape(depth, 1, a.shape[-1])

    g_pre_mix, g_post_mix, g_pre_mlp, g_post_mlp = map(vec, (g_pre_mix, g_post_mix, g_pre_mlp, g_post_mlp))

    conv_b, g_attn_out = vec(conv_b), vec(g_attn_out)
    g_head = g_mlstm_head.reshape(depth, 1, MLSTM_V_W)
    gbias = jnp.pad(jnp.concatenate([b_i, b_f], axis=1), ((0, 0), (0, GATE_LANES - 2 * nh)))
    gbias = gbias.reshape(depth, 1, GATE_LANES)

    w_in_t = jnp.swapaxes(w_in, 1, 2)
    w_in_b, w_out_b = _cast_first_layer(w_in_t, w_out, rows=64)

    for l in range(depth):
        qk, v, og, att, gc, gr, w_up_b, w_dn_b = _in_proj(
            x, mod, l, g_pre_mix, w_in_b, conv_w, conv_b, gbias, w_up, w_down, tm=256)
        hm = _mlstm(qk, v, og, gc, gr, g_head, l, chunk=256)
        ha = _swa(att, attn_sinks, g_attn_out, l, n_sub=4)
        x = _out_proj(hm, ha, w_out_b, x, mod, l, g_post_mix, tm=512)
        if l + 1 < depth:
            x, w_in_b, w_out_b = _mlp(x, mod, l, g_pre_mlp, g_post_mlp, w_up_b, w_dn_b, w_in_t, w_out,
                                      tm=512, tf=1024)
        else:
            x, = _mlp(x, mod, l, g_pre_mlp, g_post_mlp, w_up_b, w_dn_b, None, None, tm=512, tf=1024)
    return x
```

```python
import functools

import jax
import jax.numpy as jnp
from jax import lax
from jax.experimental import pallas as pl
from jax.experimental.pallas import tpu as pltpu

F32 = jnp.float32
BF16 = jnp.bfloat16

NORM_EPS = 1e-6
N_MOD = 6
CONV_WIDTH = 4
MLSTM_HEADS = 4
ATTN_HEADS = 16
KV_HEADS = 4
HEAD_DIM = 64
WINDOW = 128
GATE_LANES = 128
CARRY_ROWS = 8
NORM_ROWS = 16
NORM_UNROLL = 8

D_MODEL = 2048
MLSTM_V_W = D_MODEL // 2
MLSTM_QK_W = MLSTM_V_W // 2
ATT_Q_W = ATTN_HEADS * HEAD_DIM
ATT_KV_W = KV_HEADS * HEAD_DIM
ATT_W = ATT_Q_W + 2 * ATT_KV_W
SRC_V = 2 * MLSTM_QK_W
SRC_O = SRC_V + MLSTM_V_W
SRC_G = SRC_O + MLSTM_V_W
SRC_ATT = SRC_G + 2 * MLSTM_HEADS
SRC_END = SRC_ATT + ATT_W
PK_ATT = SRC_G
PK_G = PK_ATT + ATT_W
PK_END = PK_G + GATE_LANES
MLSTM_CHUNK = 256
GATE_VEC_ROWS = 6 * MLSTM_HEADS
PACK_COLS = 128
N_PACK_BLOCKS = PK_END // PACK_COLS

VMEM_LIMIT = 56 * 1024 * 1024

HIGHEST = lax.Precision.HIGHEST


def _sigmoid(x):
    return 1.0 / (1.0 + jnp.exp(-x))


def _log_sigmoid(x):
    return jnp.minimum(x, 0.0) - jnp.log1p(jnp.exp(-jnp.abs(x)))


def _rms(x):
    return x * lax.rsqrt(jnp.mean(x * x, axis=-1, keepdims=True) + NORM_EPS)


def _params(n_axes):
    return pltpu.CompilerParams(dimension_semantics=("arbitrary",) * n_axes,
                                vmem_limit_bytes=VMEM_LIMIT)


def _resident(shape, index_map):
    return pl.BlockSpec(shape, index_map, pipeline_mode=pl.Buffered(1))


def _ada_kernel(c_ref, w_ref, b_ref, o_ref):
    c = c_ref[...]
    ca = (c * _sigmoid(c)).astype(BF16)
    o_ref[...] = jnp.dot(ca, w_ref[...].astype(BF16), preferred_element_type=F32) + b_ref[...]


def _ada(c_pad, w_ada, b_ada):
    depth, d, n = w_ada.shape
    rows = c_pad.shape[0]
    tn = 1024
    return pl.pallas_call(
        _ada_kernel,
        grid=(depth, n // tn),
        in_specs=[
            pl.BlockSpec((rows, d), lambda l, j: (0, 0)),
            pl.BlockSpec((None, d, tn), lambda l, j: (l, 0, j)),
            pl.BlockSpec((None, 1, tn), lambda l, j: (l, 0, j)),
        ],
        out_specs=pl.BlockSpec((None, rows, tn), lambda l, j: (l, 0, j)),
        out_shape=jax.ShapeDtypeStruct((depth, rows, n), F32),
        compiler_params=_params(2),
        name="ada_mod",
    )(c_pad, w_ada, b_ada.reshape(depth, 1, n))


def _norm_modulate_rows(x_ref, g_ref, scale_ref, shift_ref, h_ref, n_rows):
    gain = g_ref[...] * (1.0 + scale_ref[...])
    shift = shift_ref[...]

    def body(r, carry):
        rows = pl.ds(pl.multiple_of(r * NORM_ROWS, NORM_ROWS), NORM_ROWS)
        h_ref[rows, :] = (_rms(x_ref[rows, :]) * gain + shift).astype(BF16)
        return carry

    lax.fori_loop(0, n_rows // NORM_ROWS, body, 0, unroll=NORM_UNROLL)


def _in_kernel(x_ref, shift_ref, scale_ref, g_ref, w_ref, convw_ref, convb_ref, gbias_ref, wup_ref, wdn_ref,
               qk_ref, v_ref, og_ref, att_ref, gv_ref, wupb_ref, wdnb_ref,
               carry_ref, buf_ref, *, tm, chunk):
    i = pl.program_id(1)
    wupb_ref[...] = wup_ref[...].astype(BF16)
    wdnb_ref[...] = wdn_ref[...].astype(BF16)

    h = _rms(x_ref[...]) * g_ref[...]
    hb = (h * (1.0 + scale_ref[...]) + shift_ref[...]).astype(BF16)

    nh = MLSTM_HEADS
    pg = jnp.dot(hb, w_ref[:, PK_G:PK_END], preferred_element_type=F32) + gbias_ref[...]
    glane = lax.broadcasted_iota(jnp.int32, pg.shape, 1)
    gt = jnp.where(glane >= nh, _log_sigmoid(pg), pg).T[0:2 * nh, :]
    t_in_chunk = lax.broadcasted_iota(jnp.int32, gt.shape, 1) & (chunk - 1)

    def scan(x, combine, identity):
        shift = 1
        while shift < chunk:
            x = combine(x, jnp.where(t_in_chunk >= shift, pltpu.roll(x, shift, axis=1), identity))
            shift *= 2
        return x

    cum = scan(gt, jnp.add, 0.0)
    b = pltpu.roll(cum, nh, axis=0)
    g = gt - b
    gv_ref[0:2 * nh, :] = g
    gv_ref[2 * nh:4 * nh, :] = scan(g, jnp.maximum, -jnp.inf)
    gv_ref[4 * nh:6 * nh, :] = b

    pq = jnp.dot(hb, w_ref[:, 0:SRC_V], preferred_element_type=F32)

    @pl.when(i == 0)
    def _():
        carry_ref[...] = jnp.zeros_like(carry_ref)

    buf_ref[0:CARRY_ROWS, :] = carry_ref[...]
    buf_ref[CARRY_ROWS:CARRY_ROWS + tm, :] = pq
    carry_ref[...] = pq[tm - CARRY_ROWS:tm, :]
    cw = convw_ref[...]
    y = cw[CONV_WIDTH - 1:CONV_WIDTH, :] * pq + convb_ref[...]
    for j in range(CONV_WIDTH - 1):
        back = CONV_WIDTH - 1 - j
        y = y + cw[j:j + 1, :] * buf_ref[CARRY_ROWS - back:CARRY_ROWS - back + tm, :]
    y = y * _sigmoid(y)
    q_scale = (MLSTM_QK_W // MLSTM_HEADS) ** -0.5
    qk_ref[:, 0:MLSTM_QK_W] = (y[:, 0:MLSTM_QK_W] * q_scale).astype(BF16)
    qk_ref[:, MLSTM_QK_W:] = y[:, MLSTM_QK_W:].astype(BF16)

    v_ref[...] = jnp.dot(hb, w_ref[:, SRC_V:SRC_O], preferred_element_type=F32).astype(BF16)
    og_ref[...] = _sigmoid(jnp.dot(hb, w_ref[:, SRC_O:SRC_G], preferred_element_type=F32)).astype(og_ref.dtype)

    pa = jnp.dot(hb, w_ref[:, PK_ATT:PK_G], preferred_element_type=F32)
    att_ref[:, 0:ATT_Q_W] = (pa[:, 0:ATT_Q_W] * (HEAD_DIM ** -0.5)).astype(BF16)
    att_ref[:, ATT_Q_W:] = pa[:, ATT_Q_W:].astype(BF16)


def _in_proj(x, mod, l, g_pre, w_packed, convw, convb, gbias, w_up, w_dn, *, tm, chunk):
    b, s, d = x.shape
    assert tm % chunk == 0 and chunk & (chunk - 1) == 0
    n_steps = b * (s // tm)
    f = w_up.shape[2]
    up_rows = d // n_steps
    dn_rows = f // n_steps

    def mod_spec(k):
        return pl.BlockSpec((None, None, None, 1, d), lambda bi, i: (l, bi, k, 0, 0))

    def w_spec(w):
        return _resident(w.shape, lambda bi, i: (0, 0))

    def out_spec(n):
        return pl.BlockSpec((None, tm, n), lambda bi, i: (bi, i, 0))

    def step(bi, i):
        return bi * (s // tm) + i

    return pl.pallas_call(
        functools.partial(_in_kernel, tm=tm, chunk=chunk),
        grid=(b, s // tm),
        in_specs=[
            pl.BlockSpec((None, tm, d), lambda bi, i: (bi, i, 0)),
            mod_spec(0), mod_spec(1),
            pl.BlockSpec((None, 1, d), lambda bi, i: (l, 0, 0)),
            w_spec(w_packed),
            pl.BlockSpec((None,) + convw.shape[1:], lambda bi, i: (l, 0, 0)),
            pl.BlockSpec((None,) + convb.shape[1:], lambda bi, i: (l, 0, 0)),
            pl.BlockSpec((None,) + gbias.shape[1:], lambda bi, i: (l, 0, 0)),
            pl.BlockSpec((None, up_rows, f), lambda bi, i: (l, step(bi, i), 0)),
            pl.BlockSpec((None, dn_rows, d), lambda bi, i: (l, step(bi, i), 0)),
        ],
        out_specs=[
            out_spec(2 * MLSTM_QK_W), out_spec(MLSTM_V_W), out_spec(MLSTM_V_W), out_spec(ATT_W),
            pl.BlockSpec((None, GATE_VEC_ROWS, tm), lambda bi, i: (bi, 0, i)),
            pl.BlockSpec((up_rows, f), lambda bi, i: (step(bi, i), 0)),
            pl.BlockSpec((dn_rows, d), lambda bi, i: (step(bi, i), 0)),
        ],
        out_shape=[
            jax.ShapeDtypeStruct((b, s, 2 * MLSTM_QK_W), BF16),
            jax.ShapeDtypeStruct((b, s, MLSTM_V_W), BF16),
            jax.ShapeDtypeStruct((b, s, MLSTM_V_W), BF16),
            jax.ShapeDtypeStruct((b, s, ATT_W), BF16),
            jax.ShapeDtypeStruct((b, GATE_VEC_ROWS, s), F32),
            jax.ShapeDtypeStruct((d, f), BF16),
            jax.ShapeDtypeStruct((f, d), BF16),
        ],
        scratch_shapes=[
            pltpu.VMEM((CARRY_ROWS, 2 * MLSTM_QK_W), F32),
            pltpu.VMEM((CARRY_ROWS + tm, 2 * MLSTM_QK_W), F32),
        ],
        compiler_params=_params(2),
        name="in_proj",
    )(x, mod, mod, g_pre, w_packed, convw, convb, gbias, w_up, w_dn)


def _mlstm_kernel(qk_ref, v_ref, og_ref, gv_ref, gh_ref, out_ref, c_ref, m_ref, *, chunk):
    j = pl.program_id(1)
    nh = MLSTM_HEADS
    dk = qk_ref.shape[1] // (2 * nh)
    dv = v_ref.shape[1] // nh

    @pl.when(j == 0)
    def _():
        c_ref[...] = jnp.zeros_like(c_ref)
        m_ref[...] = jnp.zeros_like(m_ref)

    n_chunks = qk_ref.shape[0] // chunk
    rows = 2 * nh
    key_t = lax.broadcasted_iota(jnp.int32, (chunk, chunk), 0)
    qry_t = lax.broadcasted_iota(jnp.int32, (chunk, chunk), 1)
    causal = key_t <= qry_t
    pad = jnp.zeros((GATE_LANES - 2 * rows, chunk), F32)
    nt = (((1,), (1,)), ((), ()))

    local = []
    for c in range(n_chunks):
        t0 = c * chunk
        g = gv_ref[0:rows, t0:t0 + chunk]
        cmax = gv_ref[rows:2 * rows, t0:t0 + chunk]
        g_max = cmax[:, chunk - 1:chunk]
        w = jnp.exp(g - g_max)
        key_cols = jnp.concatenate([g, w, pad], axis=0).T
        heads = []
        for h in range(nh):
            q = qk_ref[t0:t0 + chunk, h * dk:(h + 1) * dk]
            k = qk_ref[t0:t0 + chunk, (nh + h) * dk:(nh + h + 1) * dk]
            v_t = v_ref[t0:t0 + chunk, h * dv:(h + 1) * dv].astype(F32).T.astype(BF16)
            s_t = lax.dot_general(k, q, nt, preferred_element_type=F32)
            p_t = jnp.where(causal, jnp.exp(key_cols[:, h:h + 1] - cmax[h:h + 1, :]), 0.0) * s_t
            pv_t = jnp.dot(v_t, p_t.astype(BF16), preferred_element_type=F32)
            p_sum = jnp.sum(p_t, axis=0, keepdims=True)
            kw = k.astype(F32) * key_cols[:, rows + h:rows + h + 1]
            c_loc_t = jnp.dot(v_t, kw.astype(BF16), preferred_element_type=F32)
            n_loc = jnp.sum(kw, axis=0, keepdims=True)
            heads.append((q, pv_t, p_sum, c_loc_t, n_loc))
        local.append((cmax, g_max, heads))

    for c in range(n_chunks):
        t0 = c * chunk
        cmax, g_max, heads = local[c]
        b = gv_ref[2 * rows:3 * rows, t0:t0 + chunk]
        b_end = b[:, chunk - 1:chunk]
        m_prev = m_ref[:, 0:1]
        m_in = jnp.maximum(m_prev, cmax)
        f_intra = jnp.exp(cmax - m_in)
        s_inter = jnp.exp(m_prev - m_in)
        e_neg_m = jnp.exp(-(b + m_in))
        m_loc = b_end + g_max
        m_new = jnp.maximum(b_end + m_prev, m_loc)
        s_prev = jnp.exp(b_end + m_prev - m_new)
        s_loc = jnp.exp(m_loc - m_new)
        m_ref[...] = jnp.broadcast_to(m_new, m_ref.shape)
        for h in range(nh):
            q, pv_t, p_sum, c_loc_t, n_loc = heads[h]
            fi = f_intra[h:h + 1, :]
            si = s_inter[h:h + 1, :]
            q_cn_t = lax.dot_general(c_ref[h].astype(BF16), q, nt, preferred_element_type=F32)
            den = fi * p_sum + si * q_cn_t[dv:dv + 1, :]
            num_t = fi * pv_t + si * q_cn_t[0:dv, :]
            hh_t = num_t * (1.0 / jnp.maximum(jnp.abs(den), e_neg_m[h:h + 1, :]))
            hn_t = hh_t * lax.rsqrt(jnp.mean(hh_t * hh_t, axis=0, keepdims=True) + NORM_EPS)
            og = og_ref[t0:t0 + chunk, h * dv:(h + 1) * dv].astype(F32)
            out_ref[t0:t0 + chunk, h * dv:(h + 1) * dv] = (
                hn_t.T * gh_ref[:, h * dv:(h + 1) * dv] * og).astype(out_ref.dtype)
            sp = s_prev[h:h + 1, :]
            sl = s_loc[h:h + 1, :]
            c_ref[h, 0:dv, :] = sp * c_ref[h, 0:dv, :] + sl * c_loc_t
            c_ref[h, dv:, :] = sp * c_ref[h, dv:, :] + sl * n_loc


def _mlstm(qk, v, og, gr, g_head, l, *, chunk, n_chunks):
    b, s, _ = qk.shape
    nh = MLSTM_HEADS
    dk = qk.shape[2] // (2 * nh)
    dv = v.shape[2] // nh

    ts = chunk * n_chunks

    def tok_spec(n):
        return pl.BlockSpec((None, ts, n), lambda bi, j: (bi, j, 0))

    return pl.pallas_call(
        functools.partial(_mlstm_kernel, chunk=chunk),
        grid=(b, s // ts),
        in_specs=[
            tok_spec(qk.shape[2]), tok_spec(v.shape[2]), tok_spec(og.shape[2]),
            pl.BlockSpec((None, GATE_VEC_ROWS, ts), lambda bi, j: (bi, 0, j)),
            pl.BlockSpec((None, 1, v.shape[2]), lambda bi, j: (l, 0, 0)),
        ],
        out_specs=tok_spec(v.shape[2]),
        out_shape=jax.ShapeDtypeStruct((b, s, v.shape[2]), BF16),
        scratch_shapes=[
            pltpu.VMEM((nh, dv + NORM_ROWS, dk), F32),
            pltpu.VMEM((2 * nh, GATE_LANES), F32),
        ],
        compiler_params=_params(2),
        name="mlstm",
    )(qk, v, og, gr, g_head)


def _swa_kernel(sink_ref, q_ref, kvp_ref, kvc_ref, g_ref, out_ref, bias_ref, *, slopes, l):
    first = jnp.logical_and(pl.program_id(0) == 0, pl.program_id(1) == 0)
    n = pl.program_id(1)
    t = kvp_ref.shape[0]
    n_sub = q_ref.shape[0] // t
    kv_w = KV_HEADS * HEAD_DIM
    group = ATTN_HEADS // KV_HEADS
    assert t & (t - 1) == 0
    log2_t = t.bit_length() - 1
    lane_head = lax.broadcasted_iota(jnp.int32, (1, group * t), 1) >> log2_t

    def per_head(vals):
        row = jnp.full((1, group * t), vals[group - 1], F32)
        for jj in range(group - 2, -1, -1):
            row = jnp.where(lane_head <= jj, vals[jj], row)
        return row

    @pl.when(first)
    def _():
        u = lax.broadcasted_iota(jnp.int32, (2 * t, group * t), 0)
        r = lax.broadcasted_iota(jnp.int32, (2 * t, group * t), 1) & (t - 1)
        dist = r + t - u
        valid = (dist >= 0) & (dist < WINDOW)
        dist_f = dist.astype(F32)
        for g in range(KV_HEADS):
            bias = jnp.where(valid, -per_head(slopes[g * group:(g + 1) * group]) * dist_f, -jnp.inf)
            bias_ref[0, g] = bias
            bias_ref[1, g] = jnp.where(u >= t, bias, -jnp.inf)

    kv = jnp.concatenate([kvp_ref[...], kvc_ref[...]], axis=0)
    v_t = kv[:, kv_w:].astype(F32).T.astype(BF16)
    scores = []
    for sub in range(n_sub):
        for g in range(KV_HEADS):
            k = kv[sub * t:(sub + 2) * t, g * HEAD_DIM:(g + 1) * HEAD_DIM]
            qs = jnp.concatenate(
                [q_ref[sub * t:(sub + 1) * t, (g * group + jj) * HEAD_DIM:(g * group + jj + 1) * HEAD_DIM]
                 for jj in range(group)], axis=0)
            scores.append(lax.dot_general(k, qs, (((1,), (1,)), ((), ())), preferred_element_type=F32))
    for sub in range(n_sub):
        slot = jnp.where(n == 0, 1, 0) if sub == 0 else 0
        heads_t = []
        for g in range(KV_HEADS):
            s = scores[sub * KV_HEADS + g]
            probs, inv_denoms = [], []
            for jj in range(group):
                cols = slice(jj * t, (jj + 1) * t)
                sj = s[:, cols] + bias_ref[slot, g, :, cols]
                sink = sink_ref[l, g * group + jj]
                m = jnp.maximum(jnp.max(sj, axis=0, keepdims=True), sink)
                p = jnp.exp(sj - m)
                inv_denoms.append(1.0 / (jnp.sum(p, axis=0, keepdims=True) + jnp.exp(sink - m)))
                probs.append(p.astype(BF16))
            o_t = jnp.dot(v_t[g * HEAD_DIM:(g + 1) * HEAD_DIM, sub * t:(sub + 2) * t],
                          jnp.concatenate(probs, axis=1), preferred_element_type=F32)
            heads_t += [o_t[:, jj * t:(jj + 1) * t] * inv_denoms[jj] for jj in range(group)]
        o = jnp.concatenate(heads_t, axis=0).T
        out_ref[sub * t:(sub + 1) * t, :] = (_rms(o) * g_ref[...]).astype(out_ref.dtype)


def _swa(att, sinks, g_attn, l, *, n_sub):
    b, s, _ = att.shape
    t = WINDOW
    tq = n_sub * t
    q_w = ATTN_HEADS * HEAD_DIM
    kv_w2 = 2 * KV_HEADS * HEAD_DIM
    kv_blk = q_w // kv_w2
    slopes = tuple(2.0 ** (-8.0 * (h + 1) / ATTN_HEADS) for h in range(ATTN_HEADS))
    return pl.pallas_call(
        functools.partial(_swa_kernel, slopes=slopes, l=l),
        grid=(b, s // tq),
        in_specs=[
            pl.BlockSpec(memory_space=pltpu.SMEM),
            pl.BlockSpec((None, tq, q_w), lambda bi, n: (bi, n, 0)),
            pl.BlockSpec((None, t, kv_w2), lambda bi, n: (bi, jnp.maximum(n * n_sub - 1, 0), kv_blk)),
            pl.BlockSpec((None, tq, kv_w2), lambda bi, n: (bi, n, kv_blk)),
            pl.BlockSpec((None, 1, q_w), lambda bi, n: (l, 0, 0)),
        ],
        out_specs=pl.BlockSpec((None, tq, q_w), lambda bi, n: (bi, n, 0)),
        out_shape=jax.ShapeDtypeStruct((b, s, q_w), BF16),
        scratch_shapes=[
            pltpu.VMEM((2, KV_HEADS, 2 * t, (ATTN_HEADS // KV_HEADS) * t), F32),
        ],
        compiler_params=_params(2),
        name="swa",
    )(sinks, att, att, att, g_attn)


def _out_kernel(hm_ref, ha_ref, w_ref, x_ref, gate_ref, g_ref, o_ref):
    km = hm_ref.shape[1]
    y = (jnp.dot(hm_ref[...], w_ref[0:km, :], preferred_element_type=F32)
         + jnp.dot(ha_ref[...], w_ref[km:, :], preferred_element_type=F32))
    o_ref[...] = x_ref[...] + gate_ref[...] * (_rms(y) * g_ref[...])


def _out_proj(hm, ha, w_out, x, mod, l, g_post, *, tm):
    b, s, d = x.shape

    def tok_spec(n):
        return pl.BlockSpec((None, tm, n), lambda bi, i: (bi, i, 0))

    return pl.pallas_call(
        _out_kernel,
        grid=(b, s // tm),
        in_specs=[
            tok_spec(hm.shape[2]), tok_spec(ha.shape[2]),
            _resident(w_out.shape, lambda bi, i: (0, 0)),
            tok_spec(d),
            pl.BlockSpec((None, None, None, 1, d), lambda bi, i: (l, bi, 2, 0, 0)),
            pl.BlockSpec((None, 1, d), lambda bi, i: (l, 0, 0)),
        ],
        out_specs=tok_spec(d),
        out_shape=jax.ShapeDtypeStruct((b, s, d), F32),
        compiler_params=_params(2),
        name="out_proj",
    )(hm, ha, w_out, x, mod, g_post)


def _pack_src_row(k):
    tile = CARRY_ROWS
    per_blk = PACK_COLS // tile
    shifted = k * per_blk + (SRC_ATT - PK_ATT) // tile
    tiles = jnp.where(k < PK_ATT // PACK_COLS, k * per_blk, jnp.where(k < PK_G // PACK_COLS, shifted, SRC_G // tile))
    return tiles * tile


def _pack_w_in_block(k, src_ref, dst_ref):
    blk = src_ref[0]
    row = lax.broadcasted_iota(jnp.int32, blk.shape, 0)
    keep = jnp.logical_or(k < PK_G // PACK_COLS, row < 2 * MLSTM_HEADS)
    dst_ref[...] = jnp.where(keep, blk, 0.0).T.astype(BF16)


def _mlp_kernel(x_ref, shift_ref, scale_ref, gate_ref, gpre_ref, gpost_ref, wup_ref, wdn_ref, *rest,
                tm, cast_next):
    j = pl.program_id(2)
    if cast_next:
        win_ref, wout_ref, o_ref, winb_ref, woutb_ref, h_ref, acc_ref = rest
        step = (pl.program_id(0) * pl.num_programs(1) + pl.program_id(1)) * pl.num_programs(2) + j

        @pl.when(step < N_PACK_BLOCKS)
        def _():
            _pack_w_in_block(step, win_ref, winb_ref)

        woutb_ref[...] = wout_ref[...].astype(BF16)
    else:
        o_ref, h_ref, acc_ref = rest

    @pl.when(j == 0)
    def _():
        _norm_modulate_rows(x_ref, gpre_ref, scale_ref, shift_ref, h_ref, tm)
        acc_ref[...] = jnp.zeros_like(acc_ref)

    u = jnp.dot(h_ref[...], wup_ref[...], preferred_element_type=F32)
    a = jnp.square(jnp.maximum(u, 0.0)).astype(BF16)
    acc_ref[...] += jnp.dot(a, wdn_ref[...], preferred_element_type=F32)

    @pl.when(j == pl.num_programs(2) - 1)
    def _():
        gain = gate_ref[...] * gpost_ref[...]

        def body(r, carry):
            rows = pl.ds(pl.multiple_of(r * NORM_ROWS, NORM_ROWS), NORM_ROWS)
            o_ref[rows, :] = x_ref[rows, :] + gain * _rms(acc_ref[rows, :])
            return carry

        lax.fori_loop(0, tm // NORM_ROWS, body, 0, unroll=NORM_UNROLL)


def _mlp(x, mod, l, g_pre, g_post, w_up, w_dn, w_in_next, w_out_next, *, tm, tf):
    b, s, d = x.shape
    f = w_up.shape[1]
    grid = (b, s // tm, f // tf)
    cast_next = w_in_next is not None

    def mod_spec(k):
        return pl.BlockSpec((None, None, None, 1, d), lambda bi, i, j: (l, bi, k, 0, 0))

    def g_spec():
        return pl.BlockSpec((None, 1, d), lambda bi, i, j: (l, 0, 0))

    def step(bi, i, j):
        return (bi * grid[1] + i) * grid[2] + j

    tok_spec = pl.BlockSpec((None, tm, d), lambda bi, i, j: (bi, i, 0))
    in_specs = [
        tok_spec, mod_spec(3), mod_spec(4), mod_spec(5), g_spec(), g_spec(),
        pl.BlockSpec((d, tf), lambda bi, i, j: (0, j)),
        pl.BlockSpec((tf, d), lambda bi, i, j: (j, 0)),
    ]
    out_specs = [tok_spec]
    out_shape = [jax.ShapeDtypeStruct((b, s, d), F32)]
    args = [x, mod, mod, mod, g_pre, g_post, w_up, w_dn]
    if cast_next:
        n_steps = grid[0] * grid[1] * grid[2]
        assert n_steps >= N_PACK_BLOCKS
        rows = d // n_steps

        def pack_blk(bi, i, j):
            return jnp.minimum(step(bi, i, j), N_PACK_BLOCKS - 1)

        in_specs += [
            pl.BlockSpec((pl.Element(1), pl.Element(PACK_COLS), pl.Element(d)),
                         lambda bi, i, j: (l + 1, _pack_src_row(pack_blk(bi, i, j)), 0)),
            pl.BlockSpec((None, rows, d), lambda bi, i, j: (l + 1, step(bi, i, j), 0)),
        ]
        out_specs += [
            pl.BlockSpec((d, PACK_COLS), lambda bi, i, j: (0, pack_blk(bi, i, j))),
            pl.BlockSpec((rows, d), lambda bi, i, j: (step(bi, i, j), 0)),
        ]
        out_shape += [jax.ShapeDtypeStruct((d, PK_END), BF16), jax.ShapeDtypeStruct((d, d), BF16)]
        args += [w_in_next, w_out_next]

    return pl.pallas_call(
        functools.partial(_mlp_kernel, tm=tm, cast_next=cast_next),
        grid=grid,
        in_specs=in_specs,
        out_specs=out_specs,
        out_shape=out_shape,
        scratch_shapes=[pltpu.VMEM((tm, d), BF16), pltpu.VMEM((tm, d), F32)],
        compiler_params=_params(3),
        name="mlp",
    )(*args)


def _cast_kernel(win_ref, wout_ref, winb_ref, woutb_ref, *, n_out_blocks):
    t = pl.program_id(0)
    _pack_w_in_block(t, win_ref, winb_ref)

    @pl.when(t < n_out_blocks)
    def _():
        woutb_ref[...] = wout_ref[...].astype(BF16)


def _cast_first_layer(w_in_t, w_out, *, rows):
    d = w_out.shape[1]
    n_out_blocks = d // rows
    assert n_out_blocks <= N_PACK_BLOCKS

    def out_blk(t):
        return jnp.minimum(t, n_out_blocks - 1)

    return pl.pallas_call(
        functools.partial(_cast_kernel, n_out_blocks=n_out_blocks),
        grid=(N_PACK_BLOCKS,),
        in_specs=[
            pl.BlockSpec((pl.Element(1), pl.Element(PACK_COLS), pl.Element(d)),
                         lambda t: (0, _pack_src_row(t), 0)),
            pl.BlockSpec((None, rows, d), lambda t: (0, out_blk(t), 0)),
        ],
        out_specs=[
            pl.BlockSpec((d, PACK_COLS), lambda t: (0, t)),
            pl.BlockSpec((rows, d), lambda t: (out_blk(t), 0)),
        ],
        out_shape=[jax.ShapeDtypeStruct((d, PK_END), BF16), jax.ShapeDtypeStruct((d, d), BF16)],
        compiler_params=_params(1),
        name="cast_first_layer",
    )(w_in_t, w_out)


def kernel(x, c, w_ada, b_ada, g_pre_mix, g_post_mix, g_pre_mlp, g_post_mlp, w_in, conv_w, conv_b,
           b_i, b_f, g_mlstm_head, g_attn_out, attn_sinks, w_out, w_up, w_down):
    b, s, d = x.shape
    depth = w_ada.shape[0]
    nh = MLSTM_HEADS
    assert d == D_MODEL and w_in.shape[2] == SRC_END and conv_w.shape[2] == 2 * MLSTM_QK_W

    c_pad = jnp.pad(c, ((0, 8 - b), (0, 0)))
    mod = _ada(c_pad, w_ada, b_ada).reshape(depth, 8, N_MOD, 1, d)

    def vec(a):
        return a.reshape(depth, 1, a.shape[-1])

    g_pre_mix, g_post_mix, g_pre_mlp, g_post_mlp = map(vec, (g_pre_mix, g_post_mix, g_pre_mlp, g_post_mlp))

    conv_b, g_attn_out = vec(conv_b), vec(g_attn_out)
    g_head = g_mlstm_head.reshape(depth, 1, MLSTM_V_W)
    gbias = jnp.pad(jnp.concatenate([b_i, b_f], axis=1), ((0, 0), (0, GATE_LANES - 2 * nh)))
    gbias = gbias.reshape(depth, 1, GATE_LANES)

    w_in_t = jnp.swapaxes(w_in, 1, 2)
    w_in_b, w_out_b = _cast_first_layer(w_in_t, w_out, rows=64)

    for l in range(depth):
        qk, v, og, att, gr, w_up_b, w_dn_b = _in_proj(
            x, mod, l, g_pre_mix, w_in_b, conv_w, conv_b, gbias, w_up, w_down, tm=256, chunk=MLSTM_CHUNK)
        hm = _mlstm(qk, v, og, gr, g_head, l, chunk=MLSTM_CHUNK, n_chunks=4)
        ha = _swa(att, attn_sinks, g_attn_out, l, n_sub=4)
        x = _out_proj(hm, ha, w_out_b, x, mod, l, g_post_mix, tm=512)
        if l + 1 < depth:
            x, w_in_b, w_out_b = _mlp(x, mod, l, g_pre_mlp, g_post_mlp, w_up_b, w_dn_b, w_in_t, w_out,
                                      tm=512, tf=1024)
        else:
            x, = _mlp(x, mod, l, g_pre_mlp, g_post_mlp, w_up_b, w_dn_b, None, None, tm=512, tf=1024)
    return x
```

```python
import functools

import jax
import jax.numpy as jnp
from jax import lax
from jax.experimental import pallas as pl
from jax.experimental.pallas import tpu as pltpu

F32 = jnp.float32
BF16 = jnp.bfloat16

NORM_EPS = 1e-6
N_MOD = 6
CONV_WIDTH = 4
MLSTM_HEADS = 4
ATTN_HEADS = 16
KV_HEADS = 4
HEAD_DIM = 64
WINDOW = 128
GATE_LANES = 128
CARRY_ROWS = 8
NORM_ROWS = 16
NORM_UNROLL = 8

D_MODEL = 2048
MLSTM_V_W = D_MODEL // 2
MLSTM_QK_W = MLSTM_V_W // 2
ATT_Q_W = ATTN_HEADS * HEAD_DIM
ATT_KV_W = KV_HEADS * HEAD_DIM
ATT_W = ATT_Q_W + 2 * ATT_KV_W
SRC_V = 2 * MLSTM_QK_W
SRC_O = SRC_V + MLSTM_V_W
SRC_G = SRC_O + MLSTM_V_W
SRC_ATT = SRC_G + 2 * MLSTM_HEADS
SRC_END = SRC_ATT + ATT_W
PK_ATT = SRC_G
PK_G = PK_ATT + ATT_W
PK_END = PK_G + GATE_LANES
MLSTM_CHUNK = 256
GATE_VEC_ROWS = 6 * MLSTM_HEADS
PACK_COLS = 128
N_PACK_BLOCKS = PK_END // PACK_COLS

VMEM_LIMIT = 56 * 1024 * 1024

HIGHEST = lax.Precision.HIGHEST


def _sigmoid(x):
    return 1.0 / (1.0 + jnp.exp(-x))


def _log_sigmoid(x):
    return jnp.minimum(x, 0.0) - jnp.log1p(jnp.exp(-jnp.abs(x)))


def _rms(x):
    return x * lax.rsqrt(jnp.mean(x * x, axis=-1, keepdims=True) + NORM_EPS)


def _params(n_axes):
    return pltpu.CompilerParams(dimension_semantics=("arbitrary",) * n_axes,
                                vmem_limit_bytes=VMEM_LIMIT)


def _resident(shape, index_map):
    return pl.BlockSpec(shape, index_map, pipeline_mode=pl.Buffered(1))


def _ada_kernel(c_ref, w_ref, b_ref, o_ref):
    c = c_ref[...]
    ca = (c * _sigmoid(c)).astype(BF16)
    o_ref[...] = jnp.dot(ca, w_ref[...].astype(BF16), preferred_element_type=F32) + b_ref[...]


def _ada(c_pad, w_ada, b_ada):
    depth, d, n = w_ada.shape
    rows = c_pad.shape[0]
    tn = 1024
    return pl.pallas_call(
        _ada_kernel,
        grid=(depth, n // tn),
        in_specs=[
            pl.BlockSpec((rows, d), lambda l, j: (0, 0)),
            pl.BlockSpec((None, d, tn), lambda l, j: (l, 0, j)),
            pl.BlockSpec((None, 1, tn), lambda l, j: (l, 0, j)),
        ],
        out_specs=pl.BlockSpec((None, rows, tn), lambda l, j: (l, 0, j)),
        out_shape=jax.ShapeDtypeStruct((depth, rows, n), F32),
        compiler_params=_params(2),
        name="ada_mod",
    )(c_pad, w_ada, b_ada.reshape(depth, 1, n))


def _norm_modulate_rows(x_ref, g_ref, scale_ref, shift_ref, h_ref, n_rows):
    gain = g_ref[...] * (1.0 + scale_ref[...])
    shift = shift_ref[...]

    def body(r, carry):
        rows = pl.ds(pl.multiple_of(r * NORM_ROWS, NORM_ROWS), NORM_ROWS)
        h_ref[rows, :] = (_rms(x_ref[rows, :]) * gain + shift).astype(BF16)
        return carry

    lax.fori_loop(0, n_rows // NORM_ROWS, body, 0, unroll=NORM_UNROLL)


def _in_kernel(x_ref, shift_ref, scale_ref, g_ref, w_ref, convw_ref, convb_ref, gbias_ref, wup_ref, wdn_ref,
               qk_ref, v_ref, og_ref, att_ref, gv_ref, wupb_ref, wdnb_ref,
               carry_ref, buf_ref, h0_ref, h1_ref, *, tm, chunk, tiles_per_seq):
    s = pl.program_id(0)

    def produce(h_ref):
        h = _rms(x_ref[...]) * g_ref[...]
        h_ref[...] = (h * (1.0 + scale_ref[...]) + shift_ref[...]).astype(BF16)

    def consume(h_ref):
        _in_project_tile(h_ref[...], (s - 1) % tiles_per_seq == 0, w_ref, convw_ref, convb_ref, gbias_ref,
                         wup_ref, wdn_ref, qk_ref, v_ref, og_ref, att_ref, gv_ref, wupb_ref, wdnb_ref,
                         carry_ref, buf_ref, tm=tm, chunk=chunk)

    @pl.when(s == 0)
    def _():
        carry_ref[...] = jnp.zeros_like(carry_ref)
        produce(h0_ref)

    @pl.when(s % 2 == 1)
    def _():
        produce(h1_ref)
        consume(h0_ref)

    @pl.when(jnp.logical_and(s > 0, s % 2 == 0))
    def _():
        produce(h0_ref)
        consume(h1_ref)


def _in_project_tile(hb, seq_start, w_ref, convw_ref, convb_ref, gbias_ref, wup_ref, wdn_ref,
                     qk_ref, v_ref, og_ref, att_ref, gv_ref, wupb_ref, wdnb_ref, carry_ref, buf_ref, *, tm, chunk):
    wupb_ref[...] = wup_ref[...].astype(BF16)
    wdnb_ref[...] = wdn_ref[...].astype(BF16)

    nh = MLSTM_HEADS
    pg = jnp.dot(hb, w_ref[:, PK_G:PK_END], preferred_element_type=F32) + gbias_ref[...]
    glane = lax.broadcasted_iota(jnp.int32, pg.shape, 1)
    gt = jnp.where(glane >= nh, _log_sigmoid(pg), pg).T[0:2 * nh, :]
    t_in_chunk = lax.broadcasted_iota(jnp.int32, gt.shape, 1) & (chunk - 1)

    def scan(x, combine, identity):
        shift = 1
        while shift < chunk:
            x = combine(x, jnp.where(t_in_chunk >= shift, pltpu.roll(x, shift, axis=1), identity))
            shift *= 2
        return x

    cum = scan(gt, jnp.add, 0.0)
    b = pltpu.roll(cum, nh, axis=0)
    g = gt - b
    gv_ref[0:2 * nh, :] = g
    gv_ref[2 * nh:4 * nh, :] = scan(g, jnp.maximum, -jnp.inf)
    gv_ref[4 * nh:6 * nh, :] = b

    pq = jnp.dot(hb, w_ref[:, 0:SRC_V], preferred_element_type=F32)

    buf_ref[0:CARRY_ROWS, :] = jnp.where(seq_start, 0.0, carry_ref[...])
    buf_ref[CARRY_ROWS:CARRY_ROWS + tm, :] = pq
    carry_ref[...] = pq[tm - CARRY_ROWS:tm, :]
    cw = convw_ref[...]
    y = cw[CONV_WIDTH - 1:CONV_WIDTH, :] * pq + convb_ref[...]
    for j in range(CONV_WIDTH - 1):
        back = CONV_WIDTH - 1 - j
        y = y + cw[j:j + 1, :] * buf_ref[CARRY_ROWS - back:CARRY_ROWS - back + tm, :]
    y = y * _sigmoid(y)
    q_scale = (MLSTM_QK_W // MLSTM_HEADS) ** -0.5
    qk_ref[:, 0:MLSTM_QK_W] = (y[:, 0:MLSTM_QK_W] * q_scale).astype(BF16)
    qk_ref[:, MLSTM_QK_W:] = y[:, MLSTM_QK_W:].astype(BF16)

    og_ref[...] = _sigmoid(jnp.dot(hb, w_ref[:, SRC_O:SRC_G], preferred_element_type=F32)).astype(og_ref.dtype)

    pa = jnp.dot(hb, w_ref[:, PK_ATT:PK_G], preferred_element_type=F32)
    att_ref[:, 0:ATT_Q_W] = (pa[:, 0:ATT_Q_W] * (HEAD_DIM ** -0.5)).astype(BF16)
    att_ref[:, ATT_Q_W:] = pa[:, ATT_Q_W:].astype(BF16)

    v_ref[...] = jnp.dot(hb, w_ref[:, SRC_V:SRC_O], preferred_element_type=F32).astype(BF16)


def _in_proj(x, mod, l, g_pre, w_packed, convw, convb, gbias, w_up, w_dn, *, tm, chunk):
    b, s, d = x.shape
    assert tm % chunk == 0 and chunk & (chunk - 1) == 0
    tiles_per_seq = s // tm
    n_tiles = b * tiles_per_seq
    f = w_up.shape[2]
    up_rows = d // n_tiles
    dn_rows = f // n_tiles

    def norm_tile(t):
        return jnp.minimum(t, n_tiles - 1)

    def proj_tile(t):
        return jnp.maximum(t - 1, 0)

    def mod_spec(k):
        return pl.BlockSpec((None, None, None, 1, d), lambda t: (l, norm_tile(t) // tiles_per_seq, k, 0, 0))

    def out_spec(n):
        return pl.BlockSpec((None, tm, n),
                            lambda t: (proj_tile(t) // tiles_per_seq, proj_tile(t) % tiles_per_seq, 0))

    return pl.pallas_call(
        functools.partial(_in_kernel, tm=tm, chunk=chunk, tiles_per_seq=tiles_per_seq),
        grid=(n_tiles + 1,),
        in_specs=[
            pl.BlockSpec((None, tm, d),
                         lambda t: (norm_tile(t) // tiles_per_seq, norm_tile(t) % tiles_per_seq, 0)),
            mod_spec(0), mod_spec(1),
            pl.BlockSpec((None, 1, d), lambda t: (l, 0, 0)),
            _resident(w_packed.shape, lambda t: (0, 0)),
            pl.BlockSpec((None,) + convw.shape[1:], lambda t: (l, 0, 0)),
            pl.BlockSpec((None,) + convb.shape[1:], lambda t: (l, 0, 0)),
            pl.BlockSpec((None,) + gbias.shape[1:], lambda t: (l, 0, 0)),
            pl.BlockSpec((None, up_rows, f), lambda t: (l, proj_tile(t), 0)),
            pl.BlockSpec((None, dn_rows, d), lambda t: (l, proj_tile(t), 0)),
        ],
        out_specs=[
            out_spec(2 * MLSTM_QK_W), out_spec(MLSTM_V_W), out_spec(MLSTM_V_W), out_spec(ATT_W),
            pl.BlockSpec((None, GATE_VEC_ROWS, tm),
                         lambda t: (proj_tile(t) // tiles_per_seq, 0, proj_tile(t) % tiles_per_seq)),
            pl.BlockSpec((up_rows, f), lambda t: (proj_tile(t), 0)),
            pl.BlockSpec((dn_rows, d), lambda t: (proj_tile(t), 0)),
        ],
        out_shape=[
            jax.ShapeDtypeStruct((b, s, 2 * MLSTM_QK_W), BF16),
            jax.ShapeDtypeStruct((b, s, MLSTM_V_W), BF16),
            jax.ShapeDtypeStruct((b, s, MLSTM_V_W), BF16),
            jax.ShapeDtypeStruct((b, s, ATT_W), BF16),
            jax.ShapeDtypeStruct((b, GATE_VEC_ROWS, s), F32),
            jax.ShapeDtypeStruct((d, f), BF16),
            jax.ShapeDtypeStruct((f, d), BF16),
        ],
        scratch_shapes=[
            pltpu.VMEM((CARRY_ROWS, 2 * MLSTM_QK_W), F32),
            pltpu.VMEM((CARRY_ROWS + tm, 2 * MLSTM_QK_W), F32),
            pltpu.VMEM((tm, d), BF16),
            pltpu.VMEM((tm, d), BF16),
        ],
        compiler_params=_params(1),
        name="in_proj",
    )(x, mod, mod, g_pre, w_packed, convw, convb, gbias, w_up, w_dn)


def _mlstm_kernel(qk_ref, v_ref, og_ref, gv_ref, gh_ref, out_ref, c_ref, m_ref, *, chunk):
    j = pl.program_id(1)
    nh = MLSTM_HEADS
    dk = qk_ref.shape[1] // (2 * nh)
    dv = v_ref.shape[1] // nh

    @pl.when(j == 0)
    def _():
        c_ref[...] = jnp.zeros_like(c_ref)
        m_ref[...] = jnp.zeros_like(m_ref)

    n_chunks = qk_ref.shape[0] // chunk
    rows = 2 * nh
    key_t = lax.broadcasted_iota(jnp.int32, (chunk, chunk), 0)
    qry_t = lax.broadcasted_iota(jnp.int32, (chunk, chunk), 1)
    causal = key_t <= qry_t
    pad = jnp.zeros((GATE_LANES - 2 * rows, chunk), F32)
    nt = (((1,), (1,)), ((), ()))

    local = []
    for c in range(n_chunks):
        t0 = c * chunk
        g = gv_ref[0:rows, t0:t0 + chunk]
        cmax = gv_ref[rows:2 * rows, t0:t0 + chunk]
        g_max = cmax[:, chunk - 1:chunk]
        w = jnp.exp(g - g_max)
        key_cols = jnp.concatenate([g, w, pad], axis=0).T
        heads = []
        for h in range(nh):
            q = qk_ref[t0:t0 + chunk, h * dk:(h + 1) * dk]
            k = qk_ref[t0:t0 + chunk, (nh + h) * dk:(nh + h + 1) * dk]
            v_t = v_ref[t0:t0 + chunk, h * dv:(h + 1) * dv].astype(F32).T.astype(BF16)
            s_t = lax.dot_general(k, q, nt, preferred_element_type=F32)
            p_t = jnp.where(causal, jnp.exp(key_cols[:, h:h + 1] - cmax[h:h + 1, :]), 0.0) * s_t
            pv_t = jnp.dot(v_t, p_t.astype(BF16), preferred_element_type=F32)
            p_sum = jnp.sum(p_t, axis=0, keepdims=True)
            kw = k.astype(F32) * key_cols[:, rows + h:rows + h + 1]
            c_loc_t = jnp.dot(v_t, kw.astype(BF16), preferred_element_type=F32)
            n_loc = jnp.sum(kw, axis=0, keepdims=True)
            heads.append((q, pv_t, p_sum, c_loc_t, n_loc))
        local.append((cmax, g_max, heads))

    for c in range(n_chunks):
        t0 = c * chunk
        cmax, g_max, heads = local[c]
        b = gv_ref[2 * rows:3 * rows, t0:t0 + chunk]
        b_end = b[:, chunk - 1:chunk]
        m_prev = m_ref[:, 0:1]
        m_in = jnp.maximum(m_prev, cmax)
        f_intra = jnp.exp(cmax - m_in)
        s_inter = jnp.exp(m_prev - m_in)
        e_neg_m = jnp.exp(-(b + m_in))
        m_loc = b_end + g_max
        m_new = jnp.maximum(b_end + m_prev, m_loc)
        s_prev = jnp.exp(b_end + m_prev - m_new)
        s_loc = jnp.exp(m_loc - m_new)
        m_ref[...] = jnp.broadcast_to(m_new, m_ref.shape)
        for h in range(nh):
            q, pv_t, p_sum, c_loc_t, n_loc = heads[h]
            fi = f_intra[h:h + 1, :]
            si = s_inter[h:h + 1, :]
            q_cn_t = lax.dot_general(c_ref[h].astype(BF16), q, nt, preferred_element_type=F32)
            den = fi * p_sum + si * q_cn_t[dv:dv + 1, :]
            num_t = fi * pv_t + si * q_cn_t[0:dv, :]
            hh_t = num_t * (1.0 / jnp.maximum(jnp.abs(den), e_neg_m[h:h + 1, :]))
            hn_t = hh_t * lax.rsqrt(jnp.mean(hh_t * hh_t, axis=0, keepdims=True) + NORM_EPS)
            og = og_ref[t0:t0 + chunk, h * dv:(h + 1) * dv].astype(F32)
            out_ref[t0:t0 + chunk, h * dv:(h + 1) * dv] = (
                hn_t.T * gh_ref[:, h * dv:(h + 1) * dv] * og).astype(out_ref.dtype)
            sp = s_prev[h:h + 1, :]
            sl = s_loc[h:h + 1, :]
            c_ref[h, 0:dv, :] = sp * c_ref[h, 0:dv, :] + sl * c_loc_t
            c_ref[h, dv:, :] = sp * c_ref[h, dv:, :] + sl * n_loc


def _mlstm(qk, v, og, gr, g_head, l, *, chunk, n_chunks):
    b, s, _ = qk.shape
    nh = MLSTM_HEADS
    dk = qk.shape[2] // (2 * nh)
    dv = v.shape[2] // nh

    ts = chunk * n_chunks

    def tok_spec(n):
        return pl.BlockSpec((None, ts, n), lambda bi, j: (bi, j, 0))

    return pl.pallas_call(
        functools.partial(_mlstm_kernel, chunk=chunk),
        grid=(b, s // ts),
        in_specs=[
            tok_spec(qk.shape[2]), tok_spec(v.shape[2]), tok_spec(og.shape[2]),
            pl.BlockSpec((None, GATE_VEC_ROWS, ts), lambda bi, j: (bi, 0, j)),
            pl.BlockSpec((None, 1, v.shape[2]), lambda bi, j: (l, 0, 0)),
        ],
        out_specs=tok_spec(v.shape[2]),
        out_shape=jax.ShapeDtypeStruct((b, s, v.shape[2]), BF16),
        scratch_shapes=[
            pltpu.VMEM((nh, dv + NORM_ROWS, dk), F32),
            pltpu.VMEM((2 * nh, GATE_LANES), F32),
        ],
        compiler_params=_params(2),
        name="mlstm",
    )(qk, v, og, gr, g_head)


def _swa_kernel(sink_ref, q_ref, kvp_ref, kvc_ref, g_ref, out_ref, bias_ref, *, slopes, l):
    first = jnp.logical_and(pl.program_id(0) == 0, pl.program_id(1) == 0)
    n = pl.program_id(1)
    t = kvp_ref.shape[0]
    n_sub = q_ref.shape[0] // t
    kv_w = KV_HEADS * HEAD_DIM
    group = ATTN_HEADS // KV_HEADS
    assert t & (t - 1) == 0
    log2_t = t.bit_length() - 1
    lane_head = lax.broadcasted_iota(jnp.int32, (1, group * t), 1) >> log2_t

    def per_head(vals):
        row = jnp.full((1, group * t), vals[group - 1], F32)
        for jj in range(group - 2, -1, -1):
            row = jnp.where(lane_head <= jj, vals[jj], row)
        return row

    @pl.when(first)
    def _():
        u = lax.broadcasted_iota(jnp.int32, (2 * t, group * t), 0)
        r = lax.broadcasted_iota(jnp.int32, (2 * t, group * t), 1) & (t - 1)
        dist = r + t - u
        valid = (dist >= 0) & (dist < WINDOW)
        dist_f = dist.astype(F32)
        for g in range(KV_HEADS):
            bias = jnp.where(valid, -per_head(slopes[g * group:(g + 1) * group]) * dist_f, -jnp.inf)
            bias_ref[0, g] = bias
            bias_ref[1, g] = jnp.where(u >= t, bias, -jnp.inf)

    kv = jnp.concatenate([kvp_ref[...], kvc_ref[...]], axis=0)
    v_t = kv[:, kv_w:].astype(F32).T.astype(BF16)
    scores = []
    for sub in range(n_sub):
        for g in range(KV_HEADS):
            k = kv[sub * t:(sub + 2) * t, g * HEAD_DIM:(g + 1) * HEAD_DIM]
            qs = jnp.concatenate(
                [q_ref[sub * t:(sub + 1) * t, (g * group + jj) * HEAD_DIM:(g * group + jj + 1) * HEAD_DIM]
                 for jj in range(group)], axis=0)
            scores.append(lax.dot_general(k, qs, (((1,), (1,)), ((), ())), preferred_element_type=F32))
    for sub in range(n_sub):
        slot = jnp.where(n == 0, 1, 0) if sub == 0 else 0
        heads_t = []
        for g in range(KV_HEADS):
            s = scores[sub * KV_HEADS + g]
            probs, inv_denoms = [], []
            for jj in range(group):
                cols = slice(jj * t, (jj + 1) * t)
                sj = s[:, cols] + bias_ref[slot, g, :, cols]
                sink = sink_ref[l, g * group + jj]
                m = jnp.maximum(jnp.max(sj, axis=0, keepdims=True), sink)
                p = jnp.exp(sj - m)
                inv_denoms.append(1.0 / (jnp.sum(p, axis=0, keepdims=True) + jnp.exp(sink - m)))
                probs.append(p.astype(BF16))
            o_t = jnp.dot(v_t[g * HEAD_DIM:(g + 1) * HEAD_DIM, sub * t:(sub + 2) * t],
                          jnp.concatenate(probs, axis=1), preferred_element_type=F32)
            heads_t += [o_t[:, jj * t:(jj + 1) * t] * inv_denoms[jj] for jj in range(group)]
        o = jnp.concatenate(heads_t, axis=0).T
        out_ref[sub * t:(sub + 1) * t, :] = (_rms(o) * g_ref[...]).astype(out_ref.dtype)


def _swa(att, sinks, g_attn, l, *, n_sub):
    b, s, _ = att.shape
    t = WINDOW
    tq = n_sub * t
    q_w = ATTN_HEADS * HEAD_DIM
    kv_w2 = 2 * KV_HEADS * HEAD_DIM
    kv_blk = q_w // kv_w2
    slopes = tuple(2.0 ** (-8.0 * (h + 1) / ATTN_HEADS) for h in range(ATTN_HEADS))
    return pl.pallas_call(
        functools.partial(_swa_kernel, slopes=slopes, l=l),
        grid=(b, s // tq),
        in_specs=[
            pl.BlockSpec(memory_space=pltpu.SMEM),
            pl.BlockSpec((None, tq, q_w), lambda bi, n: (bi, n, 0)),
            pl.BlockSpec((None, t, kv_w2), lambda bi, n: (bi, jnp.maximum(n * n_sub - 1, 0), kv_blk)),
            pl.BlockSpec((None, tq, kv_w2), lambda bi, n: (bi, n, kv_blk)),
            pl.BlockSpec((None, 1, q_w), lambda bi, n: (l, 0, 0)),
        ],
        out_specs=pl.BlockSpec((None, tq, q_w), lambda bi, n: (bi, n, 0)),
        out_shape=jax.ShapeDtypeStruct((b, s, q_w), BF16),
        scratch_shapes=[
            pltpu.VMEM((2, KV_HEADS, 2 * t, (ATTN_HEADS // KV_HEADS) * t), F32),
        ],
        compiler_params=_params(2),
        name="swa",
    )(sinks, att, att, att, g_attn)


def _out_kernel(hm_ref, ha_ref, w_ref, x_ref, gate_ref, g_ref, o_ref):
    km = hm_ref.shape[1]
    y = (jnp.dot(hm_ref[...], w_ref[0:km, :], preferred_element_type=F32)
         + jnp.dot(ha_ref[...], w_ref[km:, :], preferred_element_type=F32))
    o_ref[...] = x_ref[...] + gate_ref[...] * (_rms(y) * g_ref[...])


def _out_proj(hm, ha, w_out, x, mod, l, g_post, *, tm):
    b, s, d = x.shape

    def tok_spec(n):
        return pl.BlockSpec((None, tm, n), lambda bi, i: (bi, i, 0))

    return pl.pallas_call(
        _out_kernel,
        grid=(b, s // tm),
        in_specs=[
            tok_spec(hm.shape[2]), tok_spec(ha.shape[2]),
            _resident(w_out.shape, lambda bi, i: (0, 0)),
            tok_spec(d),
            pl.BlockSpec((None, None, None, 1, d), lambda bi, i: (l, bi, 2, 0, 0)),
            pl.BlockSpec((None, 1, d), lambda bi, i: (l, 0, 0)),
        ],
        out_specs=tok_spec(d),
        out_shape=jax.ShapeDtypeStruct((b, s, d), F32),
        compiler_params=_params(2),
        name="out_proj",
    )(hm, ha, w_out, x, mod, g_post)


def _pack_src_row(k):
    tile = CARRY_ROWS
    per_blk = PACK_COLS // tile
    shifted = k * per_blk + (SRC_ATT - PK_ATT) // tile
    tiles = jnp.where(k < PK_ATT // PACK_COLS, k * per_blk, jnp.where(k < PK_G // PACK_COLS, shifted, SRC_G // tile))
    return tiles * tile


def _pack_w_in_block(k, src_ref, dst_ref):
    blk = src_ref[0]
    row = lax.broadcasted_iota(jnp.int32, blk.shape, 0)
    keep = jnp.logical_or(k < PK_G // PACK_COLS, row < 2 * MLSTM_HEADS)
    dst_ref[...] = jnp.where(keep, blk, 0.0).T.astype(BF16)


def _mlp_kernel(x_ref, shift_ref, scale_ref, gate_ref, gpre_ref, gpost_ref, wup_ref, wdn_ref, *rest,
                tm, cast_next):
    j = pl.program_id(2)
    if cast_next:
        win_ref, wout_ref, o_ref, winb_ref, woutb_ref, h_ref, acc_ref = rest
        step = (pl.program_id(0) * pl.num_programs(1) + pl.program_id(1)) * pl.num_programs(2) + j

        @pl.when(step < N_PACK_BLOCKS)
        def _():
            _pack_w_in_block(step, win_ref, winb_ref)

        woutb_ref[...] = wout_ref[...].astype(BF16)
    else:
        o_ref, h_ref, acc_ref = rest

    @pl.when(j == 0)
    def _():
        _norm_modulate_rows(x_ref, gpre_ref, scale_ref, shift_ref, h_ref, tm)
        acc_ref[...] = jnp.zeros_like(acc_ref)

    u = jnp.dot(h_ref[...], wup_ref[...], preferred_element_type=F32)
    a = jnp.square(jnp.maximum(u, 0.0)).astype(BF16)
    acc_ref[...] += jnp.dot(a, wdn_ref[...], preferred_element_type=F32)

    @pl.when(j == pl.num_programs(2) - 1)
    def _():
        gain = gate_ref[...] * gpost_ref[...]

        def body(r, carry):
            rows = pl.ds(pl.multiple_of(r * NORM_ROWS, NORM_ROWS), NORM_ROWS)
            o_ref[rows, :] = x_ref[rows, :] + gain * _rms(acc_ref[rows, :])
            return carry

        lax.fori_loop(0, tm // NORM_ROWS, body, 0, unroll=NORM_UNROLL)


def _mlp(x, mod, l, g_pre, g_post, w_up, w_dn, w_in_next, w_out_next, *, tm, tf):
    b, s, d = x.shape
    f = w_up.shape[1]
    grid = (b, s // tm, f // tf)
    cast_next = w_in_next is not None

    def mod_spec(k):
        return pl.BlockSpec((None, None, None, 1, d), lambda bi, i, j: (l, bi, k, 0, 0))

    def g_spec():
        return pl.BlockSpec((None, 1, d), lambda bi, i, j: (l, 0, 0))

    def step(bi, i, j):
        return (bi * grid[1] + i) * grid[2] + j

    tok_spec = pl.BlockSpec((None, tm, d), lambda bi, i, j: (bi, i, 0))
    in_specs = [
        tok_spec, mod_spec(3), mod_spec(4), mod_spec(5), g_spec(), g_spec(),
        pl.BlockSpec((d, tf), lambda bi, i, j: (0, j)),
        pl.BlockSpec((tf, d), lambda bi, i, j: (j, 0)),
    ]
    out_specs = [tok_spec]
    out_shape = [jax.ShapeDtypeStruct((b, s, d), F32)]
    args = [x, mod, mod, mod, g_pre, g_post, w_up, w_dn]
    if cast_next:
        n_steps = grid[0] * grid[1] * grid[2]
        assert n_steps >= N_PACK_BLOCKS
        rows = d // n_steps

        def pack_blk(bi, i, j):
            return jnp.minimum(step(bi, i, j), N_PACK_BLOCKS - 1)

        in_specs += [
            pl.BlockSpec((pl.Element(1), pl.Element(PACK_COLS), pl.Element(d)),
                         lambda bi, i, j: (l + 1, _pack_src_row(pack_blk(bi, i, j)), 0)),
            pl.BlockSpec((None, rows, d), lambda bi, i, j: (l + 1, step(bi, i, j), 0)),
        ]
        out_specs += [
            pl.BlockSpec((d, PACK_COLS), lambda bi, i, j: (0, pack_blk(bi, i, j))),
            pl.BlockSpec((rows, d), lambda bi, i, j: (step(bi, i, j), 0)),
        ]
        out_shape += [jax.ShapeDtypeStruct((d, PK_END), BF16), jax.ShapeDtypeStruct((d, d), BF16)]
        args += [w_in_next, w_out_next]

    return pl.pallas_call(
        functools.partial(_mlp_kernel, tm=tm, cast_next=cast_next),
        grid=grid,
        in_specs=in_specs,
        out_specs=out_specs,
        out_shape=out_shape,
        scratch_shapes=[pltpu.VMEM((tm, d), BF16), pltpu.VMEM((tm, d), F32)],
        compiler_params=_params(3),
        name="mlp",
    )(*args)


def _cast_kernel(win_ref, wout_ref, winb_ref, woutb_ref, *, n_out_blocks):
    t = pl.program_id(0)
    _pack_w_in_block(t, win_ref, winb_ref)

    @pl.when(t < n_out_blocks)
    def _():
        woutb_ref[...] = wout_ref[...].astype(BF16)


def _cast_first_layer(w_in_t, w_out, *, rows):
    d = w_out.shape[1]
    n_out_blocks = d // rows
    assert n_out_blocks <= N_PACK_BLOCKS

    def out_blk(t):
        return jnp.minimum(t, n_out_blocks - 1)

    return pl.pallas_call(
        functools.partial(_cast_kernel, n_out_blocks=n_out_blocks),
        grid=(N_PACK_BLOCKS,),
        in_specs=[
            pl.BlockSpec((pl.Element(1), pl.Element(PACK_COLS), pl.Element(d)),
                         lambda t: (0, _pack_src_row(t), 0)),
            pl.BlockSpec((None, rows, d), lambda t: (0, out_blk(t), 0)),
        ],
        out_specs=[
            pl.BlockSpec((d, PACK_COLS), lambda t: (0, t)),
            pl.BlockSpec((rows, d), lambda t: (out_blk(t), 0)),
        ],
        out_shape=[jax.ShapeDtypeStruct((d, PK_END), BF16), jax.ShapeDtypeStruct((d, d), BF16)],
        compiler_params=_params(1),
        name="cast_first_layer",
    )(w_in_t, w_out)


def kernel(x, c, w_ada, b_ada, g_pre_mix, g_post_mix, g_pre_mlp, g_post_mlp, w_in, conv_w, conv_b,
           b_i, b_f, g_mlstm_head, g_attn_out, attn_sinks, w_out, w_up, w_down):
    b, s, d = x.shape
    depth = w_ada.shape[0]
    nh = MLSTM_HEADS
    assert d == D_MODEL and w_in.shape[2] == SRC_END and conv_w.shape[2] == 2 * MLSTM_QK_W

    c_pad = jnp.pad(c, ((0, 8 - b), (0, 0)))
    mod = _ada(c_pad, w_ada, b_ada).reshape(depth, 8, N_MOD, 1, d)

    def vec(a):
        return a.reshape(depth, 1, a.shape[-1])

    g_pre_mix, g_post_mix, g_pre_mlp, g_post_mlp = map(vec, (g_pre_mix, g_post_mix, g_pre_mlp, g_post_mlp))

    conv_b, g_attn_out = vec(conv_b), vec(g_attn_out)
    g_head = g_mlstm_head.reshape(depth, 1, MLSTM_V_W)
    gbias = jnp.pad(jnp.concatenate([b_i, b_f], axis=1), ((0, 0), (0, GATE_LANES - 2 * nh)))
    gbias = gbias.reshape(depth, 1, GATE_LANES)

    w_in_t = jnp.swapaxes(w_in, 1, 2)
    w_in_b, w_out_b = _cast_first_layer(w_in_t, w_out, rows=64)

    for l in range(depth):
        qk, v, og, att, gr, w_up_b, w_dn_b = _in_proj(
            x, mod, l, g_pre_mix, w_in_b, conv_w, conv_b, gbias, w_up, w_down, tm=256, chunk=MLSTM_CHUNK)
        hm = _mlstm(qk, v, og, gr, g_head, l, chunk=MLSTM_CHUNK, n_chunks=4)
        ha = _swa(att, attn_sinks, g_attn_out, l, n_sub=4)
        x = _out_proj(hm, ha, w_out_b, x, mod, l, g_post_mix, tm=512)
        if l + 1 < depth:
            x, w_in_b, w_out_b = _mlp(x, mod, l, g_pre_mlp, g_post_mlp, w_up_b, w_dn_b, w_in_t, w_out,
                                      tm=512, tf=1024)
        else:
            x, = _mlp(x, mod, l, g_pre_mlp, g_post_mlp, w_up_b, w_dn_b, None, None, tm=512, tf=1024)
    return x
```

```python
import functools

import jax
import jax.numpy as jnp
from jax import lax
from jax.experimental import pallas as pl
from jax.experimental.pallas import tpu as pltpu

F32 = jnp.float32
BF16 = jnp.bfloat16

NORM_EPS = 1e-6
N_MOD = 6
CONV_WIDTH = 4
MLSTM_HEADS = 4
ATTN_HEADS = 16
KV_HEADS = 4
HEAD_DIM = 64
WINDOW = 128
GATE_LANES = 128
CARRY_ROWS = 8
NORM_ROWS = 16
NORM_UNROLL = 8

D_MODEL = 2048
MLSTM_V_W = D_MODEL // 2
MLSTM_QK_W = MLSTM_V_W // 2
ATT_Q_W = ATTN_HEADS * HEAD_DIM
ATT_KV_W = KV_HEADS * HEAD_DIM
ATT_W = ATT_Q_W + 2 * ATT_KV_W
SRC_V = 2 * MLSTM_QK_W
SRC_O = SRC_V + MLSTM_V_W
SRC_G = SRC_O + MLSTM_V_W
SRC_ATT = SRC_G + 2 * MLSTM_HEADS
SRC_END = SRC_ATT + ATT_W
PK_ATT = SRC_G
PK_G = PK_ATT + ATT_W
PK_END = PK_G + GATE_LANES
MLSTM_CHUNK = 256
GATE_VEC_ROWS = 6 * MLSTM_HEADS
PACK_COLS = 128
N_PACK_BLOCKS = PK_END // PACK_COLS

VMEM_LIMIT = 56 * 1024 * 1024

HIGHEST = lax.Precision.HIGHEST


def _sigmoid(x):
    return 1.0 / (1.0 + jnp.exp(-x))


def _log_sigmoid(x):
    return jnp.minimum(x, 0.0) - jnp.log1p(jnp.exp(-jnp.abs(x)))


def _rms(x):
    return x * lax.rsqrt(jnp.mean(x * x, axis=-1, keepdims=True) + NORM_EPS)


def _params(n_axes):
    return pltpu.CompilerParams(dimension_semantics=("arbitrary",) * n_axes,
                                vmem_limit_bytes=VMEM_LIMIT)


def _resident(shape, index_map):
    return pl.BlockSpec(shape, index_map, pipeline_mode=pl.Buffered(1))


def _ada_kernel(c_ref, w_ref, b_ref, o_ref):
    c = c_ref[...]
    ca = (c * _sigmoid(c)).astype(BF16)
    o_ref[...] = jnp.dot(ca, w_ref[...].astype(BF16), preferred_element_type=F32) + b_ref[...]


def _ada(c_pad, w_ada, b_ada):
    depth, d, n = w_ada.shape
    rows = c_pad.shape[0]
    tn = 1024
    return pl.pallas_call(
        _ada_kernel,
        grid=(depth, n // tn),
        in_specs=[
            pl.BlockSpec((rows, d), lambda l, j: (0, 0)),
            pl.BlockSpec((None, d, tn), lambda l, j: (l, 0, j)),
            pl.BlockSpec((None, 1, tn), lambda l, j: (l, 0, j)),
        ],
        out_specs=pl.BlockSpec((None, rows, tn), lambda l, j: (l, 0, j)),
        out_shape=jax.ShapeDtypeStruct((depth, rows, n), F32),
        compiler_params=_params(2),
        name="ada_mod",
    )(c_pad, w_ada, b_ada.reshape(depth, 1, n))


def _norm_modulate_rows(x_ref, g_ref, scale_ref, shift_ref, h_ref, n_rows):
    gain = g_ref[...] * (1.0 + scale_ref[...])
    shift = shift_ref[...]

    def body(r, carry):
        rows = pl.ds(pl.multiple_of(r * NORM_ROWS, NORM_ROWS), NORM_ROWS)
        h_ref[rows, :] = (_rms(x_ref[rows, :]) * gain + shift).astype(BF16)
        return carry

    lax.fori_loop(0, n_rows // NORM_ROWS, body, 0, unroll=NORM_UNROLL)


def _in_kernel(x_ref, shift_ref, scale_ref, g_ref, w_ref, convw_ref, convb_ref, gbias_ref, wup_ref, wdn_ref,
               qk_ref, v_ref, og_ref, att_ref, gv_ref, wupb_ref, wdnb_ref,
               carry_ref, buf_ref, h0_ref, h1_ref, *, tm, chunk, tiles_per_seq):
    s = pl.program_id(0)

    def produce(h_ref):
        h = _rms(x_ref[...]) * g_ref[...]
        h_ref[...] = (h * (1.0 + scale_ref[...]) + shift_ref[...]).astype(BF16)

    def consume(h_ref):
        _in_project_tile(h_ref[...], (s - 1) % tiles_per_seq == 0, w_ref, convw_ref, convb_ref, gbias_ref,
                         wup_ref, wdn_ref, qk_ref, v_ref, og_ref, att_ref, gv_ref, wupb_ref, wdnb_ref,
                         carry_ref, buf_ref, tm=tm, chunk=chunk)

    @pl.when(s == 0)
    def _():
        carry_ref[...] = jnp.zeros_like(carry_ref)
        produce(h0_ref)

    @pl.when(s % 2 == 1)
    def _():
        produce(h1_ref)
        consume(h0_ref)

    @pl.when(jnp.logical_and(s > 0, s % 2 == 0))
    def _():
        produce(h0_ref)
        consume(h1_ref)


def _in_project_tile(hb, seq_start, w_ref, convw_ref, convb_ref, gbias_ref, wup_ref, wdn_ref,
                     qk_ref, v_ref, og_ref, att_ref, gv_ref, wupb_ref, wdnb_ref, carry_ref, buf_ref, *, tm, chunk):
    wupb_ref[...] = wup_ref[...].astype(BF16)
    wdnb_ref[...] = wdn_ref[...].astype(BF16)

    nh = MLSTM_HEADS
    pg = jnp.dot(hb, w_ref[:, PK_G:PK_END], preferred_element_type=F32) + gbias_ref[...]
    glane = lax.broadcasted_iota(jnp.int32, pg.shape, 1)
    gt = jnp.where(glane >= nh, _log_sigmoid(pg), pg).T[0:2 * nh, :]
    t_in_chunk = lax.broadcasted_iota(jnp.int32, gt.shape, 1) & (chunk - 1)

    def scan(x, combine, identity):
        shift = 1
        while shift < chunk:
            x = combine(x, jnp.where(t_in_chunk >= shift, pltpu.roll(x, shift, axis=1), identity))
            shift *= 2
        return x

    cum = scan(gt, jnp.add, 0.0)
    b = pltpu.roll(cum, nh, axis=0)
    g = gt - b
    gv_ref[0:2 * nh, :] = g
    gv_ref[2 * nh:4 * nh, :] = scan(g, jnp.maximum, -jnp.inf)
    gv_ref[4 * nh:6 * nh, :] = b

    pq = jnp.dot(hb, w_ref[:, 0:SRC_V], preferred_element_type=F32)

    buf_ref[0:CARRY_ROWS, :] = jnp.where(seq_start, 0.0, carry_ref[...])
    buf_ref[CARRY_ROWS:CARRY_ROWS + tm, :] = pq
    carry_ref[...] = pq[tm - CARRY_ROWS:tm, :]
    cw = convw_ref[...]
    y = cw[CONV_WIDTH - 1:CONV_WIDTH, :] * pq + convb_ref[...]
    for j in range(CONV_WIDTH - 1):
        back = CONV_WIDTH - 1 - j
        y = y + cw[j:j + 1, :] * buf_ref[CARRY_ROWS - back:CARRY_ROWS - back + tm, :]
    y = y * _sigmoid(y)
    q_scale = (MLSTM_QK_W // MLSTM_HEADS) ** -0.5
    qk_ref[:, 0:MLSTM_QK_W] = (y[:, 0:MLSTM_QK_W] * q_scale).astype(BF16)
    qk_ref[:, MLSTM_QK_W:] = y[:, MLSTM_QK_W:].astype(BF16)

    og_ref[...] = _sigmoid(jnp.dot(hb, w_ref[:, SRC_O:SRC_G], preferred_element_type=F32)).astype(og_ref.dtype)

    pa = jnp.dot(hb, w_ref[:, PK_ATT:PK_G], preferred_element_type=F32)
    att_ref[:, 0:ATT_Q_W] = (pa[:, 0:ATT_Q_W] * (HEAD_DIM ** -0.5)).astype(BF16)
    att_ref[:, ATT_Q_W:] = pa[:, ATT_Q_W:].astype(BF16)

    v_ref[...] = jnp.dot(hb, w_ref[:, SRC_V:SRC_O], preferred_element_type=F32).astype(BF16)


def _in_proj(x, mod, l, g_pre, w_packed, convw, convb, gbias, w_up, w_dn, *, tm, chunk):
    b, s, d = x.shape
    assert tm % chunk == 0 and chunk & (chunk - 1) == 0
    tiles_per_seq = s // tm
    n_tiles = b * tiles_per_seq
    f = w_up.shape[2]
    up_rows = d // n_tiles
    dn_rows = f // n_tiles

    def norm_tile(t):
        return jnp.minimum(t, n_tiles - 1)

    def proj_tile(t):
        return jnp.maximum(t - 1, 0)

    def mod_spec(k):
        return pl.BlockSpec((None, None, None, 1, d), lambda t: (l, norm_tile(t) // tiles_per_seq, k, 0, 0))

    def out_spec(n):
        return pl.BlockSpec((None, tm, n),
                            lambda t: (proj_tile(t) // tiles_per_seq, proj_tile(t) % tiles_per_seq, 0))

    return pl.pallas_call(
        functools.partial(_in_kernel, tm=tm, chunk=chunk, tiles_per_seq=tiles_per_seq),
        grid=(n_tiles + 1,),
        in_specs=[
            pl.BlockSpec((None, tm, d),
                         lambda t: (norm_tile(t) // tiles_per_seq, norm_tile(t) % tiles_per_seq, 0)),
            mod_spec(0), mod_spec(1),
            pl.BlockSpec((None, 1, d), lambda t: (l, 0, 0)),
            _resident(w_packed.shape, lambda t: (0, 0)),
            pl.BlockSpec((None,) + convw.shape[1:], lambda t: (l, 0, 0)),
            pl.BlockSpec((None,) + convb.shape[1:], lambda t: (l, 0, 0)),
            pl.BlockSpec((None,) + gbias.shape[1:], lambda t: (l, 0, 0)),
            pl.BlockSpec((None, up_rows, f), lambda t: (l, proj_tile(t), 0)),
            pl.BlockSpec((None, dn_rows, d), lambda t: (l, proj_tile(t), 0)),
        ],
        out_specs=[
            out_spec(2 * MLSTM_QK_W), out_spec(MLSTM_V_W), out_spec(MLSTM_V_W), out_spec(ATT_W),
            pl.BlockSpec((None, GATE_VEC_ROWS, tm),
                         lambda t: (proj_tile(t) // tiles_per_seq, 0, proj_tile(t) % tiles_per_seq)),
            pl.BlockSpec((up_rows, f), lambda t: (proj_tile(t), 0)),
            pl.BlockSpec((dn_rows, d), lambda t: (proj_tile(t), 0)),
        ],
        out_shape=[
            jax.ShapeDtypeStruct((b, s, 2 * MLSTM_QK_W), BF16),
            jax.ShapeDtypeStruct((b, s, MLSTM_V_W), BF16),
            jax.ShapeDtypeStruct((b, s, MLSTM_V_W), BF16),
            jax.ShapeDtypeStruct((b, s, ATT_W), BF16),
            jax.ShapeDtypeStruct((b, GATE_VEC_ROWS, s), F32),
            jax.ShapeDtypeStruct((d, f), BF16),
            jax.ShapeDtypeStruct((f, d), BF16),
        ],
        scratch_shapes=[
            pltpu.VMEM((CARRY_ROWS, 2 * MLSTM_QK_W), F32),
            pltpu.VMEM((CARRY_ROWS + tm, 2 * MLSTM_QK_W), F32),
            pltpu.VMEM((tm, d), BF16),
            pltpu.VMEM((tm, d), BF16),
        ],
        compiler_params=_params(1),
        name="in_proj",
    )(x, mod, mod, g_pre, w_packed, convw, convb, gbias, w_up, w_dn)


def _mlstm_kernel(qk_ref, v_ref, og_ref, gv_ref, gh_ref, out_ref, c_ref, m_ref, *, chunk):
    j = pl.program_id(1)
    nh = MLSTM_HEADS
    dk = qk_ref.shape[1] // (2 * nh)
    dv = v_ref.shape[1] // nh

    @pl.when(j == 0)
    def _():
        c_ref[...] = jnp.zeros_like(c_ref)
        m_ref[...] = jnp.zeros_like(m_ref)

    n_chunks = qk_ref.shape[0] // chunk
    rows = 2 * nh
    key_t = lax.broadcasted_iota(jnp.int32, (chunk, chunk), 0)
    qry_t = lax.broadcasted_iota(jnp.int32, (chunk, chunk), 1)
    causal = key_t <= qry_t
    pad = jnp.zeros((GATE_LANES - 2 * rows, chunk), F32)
    nt = (((1,), (1,)), ((), ()))

    local = []
    for c in range(n_chunks):
        t0 = c * chunk
        g = gv_ref[0:rows, t0:t0 + chunk]
        cmax = gv_ref[rows:2 * rows, t0:t0 + chunk]
        g_max = cmax[:, chunk - 1:chunk]
        w = jnp.exp(g - g_max)
        key_cols = jnp.concatenate([g, w, pad], axis=0).T
        heads = []
        for h in range(nh):
            q = qk_ref[t0:t0 + chunk, h * dk:(h + 1) * dk]
            k = qk_ref[t0:t0 + chunk, (nh + h) * dk:(nh + h + 1) * dk]
            v_t = v_ref[t0:t0 + chunk, h * dv:(h + 1) * dv].astype(F32).T.astype(BF16)
            s_t = lax.dot_general(k, q, nt, preferred_element_type=F32)
            p_t = jnp.where(causal, jnp.exp(key_cols[:, h:h + 1] - cmax[h:h + 1, :]), 0.0) * s_t
            pv_t = jnp.dot(v_t, p_t.astype(BF16), preferred_element_type=F32)
            p_sum = jnp.sum(p_t, axis=0, keepdims=True)
            kw = k.astype(F32) * key_cols[:, rows + h:rows + h + 1]
            c_loc_t = jnp.dot(v_t, kw.astype(BF16), preferred_element_type=F32)
            n_loc = jnp.sum(kw, axis=0, keepdims=True)
            heads.append((q, pv_t, p_sum, c_loc_t, n_loc))
        local.append((cmax, g_max, heads))

    for c in range(n_chunks):
        t0 = c * chunk
        cmax, g_max, heads = local[c]
        b = gv_ref[2 * rows:3 * rows, t0:t0 + chunk]
        b_end = b[:, chunk - 1:chunk]
        m_prev = m_ref[:, 0:1]
        m_in = jnp.maximum(m_prev, cmax)
        f_intra = jnp.exp(cmax - m_in)
        s_inter = jnp.exp(m_prev - m_in)
        e_neg_m = jnp.exp(-(b + m_in))
        m_loc = b_end + g_max
        m_new = jnp.maximum(b_end + m_prev, m_loc)
        s_prev = jnp.exp(b_end + m_prev - m_new)
        s_loc = jnp.exp(m_loc - m_new)
        m_ref[...] = jnp.broadcast_to(m_new, m_ref.shape)
        for h in range(nh):
            q, pv_t, p_sum, c_loc_t, n_loc = heads[h]
            fi = f_intra[h:h + 1, :]
            si = s_inter[h:h + 1, :]
            q_cn_t = lax.dot_general(c_ref[h].astype(BF16), q, nt, preferred_element_type=F32)
            den = fi * p_sum + si * q_cn_t[dv:dv + 1, :]
            num_t = fi * pv_t + si * q_cn_t[0:dv, :]
            hh_t = num_t * (1.0 / jnp.maximum(jnp.abs(den), e_neg_m[h:h + 1, :]))
            hn_t = hh_t * lax.rsqrt(jnp.mean(hh_t * hh_t, axis=0, keepdims=True) + NORM_EPS)
            og = og_ref[t0:t0 + chunk, h * dv:(h + 1) * dv].astype(F32)
            out_ref[t0:t0 + chunk, h * dv:(h + 1) * dv] = (
                hn_t.T * gh_ref[:, h * dv:(h + 1) * dv] * og).astype(out_ref.dtype)
            sp = s_prev[h:h + 1, :]
            sl = s_loc[h:h + 1, :]
            c_ref[h, 0:dv, :] = sp * c_ref[h, 0:dv, :] + sl * c_loc_t
            c_ref[h, dv:, :] = sp * c_ref[h, dv:, :] + sl * n_loc


def _mlstm(qk, v, og, gr, g_head, l, *, chunk, n_chunks):
    b, s, _ = qk.shape
    nh = MLSTM_HEADS
    dk = qk.shape[2] // (2 * nh)
    dv = v.shape[2] // nh

    ts = chunk * n_chunks

    def tok_spec(n):
        return pl.BlockSpec((None, ts, n), lambda bi, j: (bi, j, 0))

    return pl.pallas_call(
        functools.partial(_mlstm_kernel, chunk=chunk),
        grid=(b, s // ts),
        in_specs=[
            tok_spec(qk.shape[2]), tok_spec(v.shape[2]), tok_spec(og.shape[2]),
            pl.BlockSpec((None, GATE_VEC_ROWS, ts), lambda bi, j: (bi, 0, j)),
            pl.BlockSpec((None, 1, v.shape[2]), lambda bi, j: (l, 0, 0)),
        ],
        out_specs=tok_spec(v.shape[2]),
        out_shape=jax.ShapeDtypeStruct((b, s, v.shape[2]), BF16),
        scratch_shapes=[
            pltpu.VMEM((nh, dv + NORM_ROWS, dk), F32),
            pltpu.VMEM((2 * nh, GATE_LANES), F32),
        ],
        compiler_params=_params(2),
        name="mlstm",
    )(qk, v, og, gr, g_head)


def _swa_kernel(sink_ref, q_ref, kvp_ref, kvc_ref, g_ref, out_ref, bias_ref, *, slopes, l):
    first = jnp.logical_and(pl.program_id(0) == 0, pl.program_id(1) == 0)
    n = pl.program_id(1)
    t = kvp_ref.shape[0]
    n_sub = q_ref.shape[0] // t
    kv_w = KV_HEADS * HEAD_DIM
    group = ATTN_HEADS // KV_HEADS
    assert t & (t - 1) == 0
    log2_t = t.bit_length() - 1
    lane_head = lax.broadcasted_iota(jnp.int32, (1, group * t), 1) >> log2_t

    def per_head(vals):
        row = jnp.full((1, group * t), vals[group - 1], F32)
        for jj in range(group - 2, -1, -1):
            row = jnp.where(lane_head <= jj, vals[jj], row)
        return row

    @pl.when(first)
    def _():
        u = lax.broadcasted_iota(jnp.int32, (2 * t, group * t), 0)
        r = lax.broadcasted_iota(jnp.int32, (2 * t, group * t), 1) & (t - 1)
        dist = r + t - u
        valid = (dist >= 0) & (dist < WINDOW)
        dist_f = dist.astype(F32)
        for g in range(KV_HEADS):
            bias = jnp.where(valid, -per_head(slopes[g * group:(g + 1) * group]) * dist_f, -jnp.inf)
            bias_ref[0, g] = bias
            bias_ref[1, g] = jnp.where(u >= t, bias, -jnp.inf)

    kv = jnp.concatenate([kvp_ref[...], kvc_ref[...]], axis=0)
    v_t = kv[:, kv_w:].astype(F32).T.astype(BF16)
    scores = []
    for sub in range(n_sub):
        for g in range(KV_HEADS):
            k = kv[sub * t:(sub + 2) * t, g * HEAD_DIM:(g + 1) * HEAD_DIM]
            qs = jnp.concatenate(
                [q_ref[sub * t:(sub + 1) * t, (g * group + jj) * HEAD_DIM:(g * group + jj + 1) * HEAD_DIM]
                 for jj in range(group)], axis=0)
            scores.append(lax.dot_general(k, qs, (((1,), (1,)), ((), ())), preferred_element_type=F32))
    for sub in range(n_sub):
        slot = jnp.where(n == 0, 1, 0) if sub == 0 else 0
        heads_t = []
        for g in range(KV_HEADS):
            s = scores[sub * KV_HEADS + g]
            probs, inv_denoms = [], []
            for jj in range(group):
                cols = slice(jj * t, (jj + 1) * t)
                sj = s[:, cols] + bias_ref[slot, g, :, cols]
                sink = sink_ref[l, g * group + jj]
                m = jnp.maximum(jnp.max(sj, axis=0, keepdims=True), sink)
                p = jnp.exp(sj - m)
                inv_denoms.append(1.0 / (jnp.sum(p, axis=0, keepdims=True) + jnp.exp(sink - m)))
                probs.append(p.astype(BF16))
            o_t = jnp.dot(v_t[g * HEAD_DIM:(g + 1) * HEAD_DIM, sub * t:(sub + 2) * t],
                          jnp.concatenate(probs, axis=1), preferred_element_type=F32)
            heads_t += [o_t[:, jj * t:(jj + 1) * t] * inv_denoms[jj] for jj in range(group)]
        o = jnp.concatenate(heads_t, axis=0).T
        out_ref[sub * t:(sub + 1) * t, :] = (_rms(o) * g_ref[...]).astype(out_ref.dtype)


def _swa(att, sinks, g_attn, l, *, n_sub):
    b, s, _ = att.shape
    t = WINDOW
    tq = n_sub * t
    q_w = ATTN_HEADS * HEAD_DIM
    kv_w2 = 2 * KV_HEADS * HEAD_DIM
    kv_blk = q_w // kv_w2
    slopes = tuple(2.0 ** (-8.0 * (h + 1) / ATTN_HEADS) for h in range(ATTN_HEADS))
    return pl.pallas_call(
        functools.partial(_swa_kernel, slopes=slopes, l=l),
        grid=(b, s // tq),
        in_specs=[
            pl.BlockSpec(memory_space=pltpu.SMEM),
            pl.BlockSpec((None, tq, q_w), lambda bi, n: (bi, n, 0)),
            pl.BlockSpec((None, t, kv_w2), lambda bi, n: (bi, jnp.maximum(n * n_sub - 1, 0), kv_blk)),
            pl.BlockSpec((None, tq, kv_w2), lambda bi, n: (bi, n, kv_blk)),
            pl.BlockSpec((None, 1, q_w), lambda bi, n: (l, 0, 0)),
        ],
        out_specs=pl.BlockSpec((None, tq, q_w), lambda bi, n: (bi, n, 0)),
        out_shape=jax.ShapeDtypeStruct((b, s, q_w), BF16),
        scratch_shapes=[
            pltpu.VMEM((2, KV_HEADS, 2 * t, (ATTN_HEADS // KV_HEADS) * t), F32),
        ],
        compiler_params=_params(2),
        name="swa",
    )(sinks, att, att, att, g_attn)


def _out_kernel(hm_ref, ha_ref, w_ref, x_ref, gate_ref, g_ref, o_ref):
    km = hm_ref.shape[1]
    y = (jnp.dot(hm_ref[...], w_ref[0:km, :], preferred_element_type=F32)
         + jnp.dot(ha_ref[...], w_ref[km:, :], preferred_element_type=F32))
    o_ref[...] = x_ref[...] + gate_ref[...] * (_rms(y) * g_ref[...])


def _out_proj(hm, ha, w_out, x, mod, l, g_post, *, tm):
    b, s, d = x.shape

    def tok_spec(n):
        return pl.BlockSpec((None, tm, n), lambda bi, i: (bi, i, 0))

    return pl.pallas_call(
        _out_kernel,
        grid=(b, s // tm),
        in_specs=[
            tok_spec(hm.shape[2]), tok_spec(ha.shape[2]),
            _resident(w_out.shape, lambda bi, i: (0, 0)),
            tok_spec(d),
            pl.BlockSpec((None, None, None, 1, d), lambda bi, i: (l, bi, 2, 0, 0)),
            pl.BlockSpec((None, 1, d), lambda bi, i: (l, 0, 0)),
        ],
        out_specs=tok_spec(d),
        out_shape=jax.ShapeDtypeStruct((b, s, d), F32),
        compiler_params=_params(2),
        name="out_proj",
    )(hm, ha, w_out, x, mod, g_post)


def _pack_src_row(k):
    tile = CARRY_ROWS
    per_blk = PACK_COLS // tile
    shifted = k * per_blk + (SRC_ATT - PK_ATT) // tile
    tiles = jnp.where(k < PK_ATT // PACK_COLS, k * per_blk, jnp.where(k < PK_G // PACK_COLS, shifted, SRC_G // tile))
    return tiles * tile


def _pack_w_in_block(k, src_ref, dst_ref):
    blk = src_ref[0]
    row = lax.broadcasted_iota(jnp.int32, blk.shape, 0)
    keep = jnp.logical_or(k < PK_G // PACK_COLS, row < 2 * MLSTM_HEADS)
    dst_ref[...] = jnp.where(keep, blk, 0.0).T.astype(BF16)


def _mlp_kernel(xn_ref, xp_ref, shift_ref, scale_ref, gate_ref, gpre_ref, gpost_ref, wup_ref, wdn_ref, *rest,
                n_tiles, rows, cast_next, n_out_blocks):
    p = pl.program_id(0)
    j = pl.program_id(1)
    if cast_next:
        win_ref, wout_ref, o_ref, winb_ref, woutb_ref, h0_ref, h1_ref, acc0_ref, acc1_ref = rest
        step = p * pl.num_programs(1) + j

        @pl.when(step < N_PACK_BLOCKS)
        def _():
            _pack_w_in_block(step, win_ref, winb_ref)

        @pl.when(step < n_out_blocks)
        def _():
            woutb_ref[...] = wout_ref[...].astype(BF16)
    else:
        o_ref, h0_ref, h1_ref, acc0_ref, acc1_ref = rest
    sl = pl.ds(pl.multiple_of(j * rows, rows), rows)

    def matmuls(h_ref, acc_ref):
        u = jnp.dot(h_ref[...], wup_ref[...], preferred_element_type=F32)
        a = jnp.square(jnp.maximum(u, 0.0)).astype(BF16)
        acc_ref[...] += jnp.dot(a, wdn_ref[...], preferred_element_type=F32)

    def norm_slice(h_ref):
        h = _rms(xn_ref[...]) * gpre_ref[...]
        h_ref[sl, :] = (h * (1.0 + scale_ref[...]) + shift_ref[...]).astype(BF16)

    def finish_slice(acc_ref):
        o_ref[...] = xp_ref[...] + gate_ref[...] * (_rms(acc_ref[sl, :]) * gpost_ref[...])
        acc_ref[sl, :] = jnp.zeros((rows, acc_ref.shape[1]), F32)

    @pl.when(p == 0)
    def _():
        norm_slice(h0_ref)
        acc0_ref[sl, :] = jnp.zeros((rows, acc0_ref.shape[1]), F32)
        acc1_ref[sl, :] = jnp.zeros((rows, acc1_ref.shape[1]), F32)

    @pl.when(p == 1)
    def _():
        norm_slice(h1_ref)
        matmuls(h0_ref, acc0_ref)

    in_steady = jnp.logical_and(p >= 2, p <= n_tiles)

    @pl.when(jnp.logical_and(in_steady, p % 2 == 0))
    def _():
        norm_slice(h0_ref)
        finish_slice(acc0_ref)
        matmuls(h1_ref, acc1_ref)

    @pl.when(jnp.logical_and(in_steady, p % 2 == 1))
    def _():
        norm_slice(h1_ref)
        finish_slice(acc1_ref)
        matmuls(h0_ref, acc0_ref)

    @pl.when(p == n_tiles + 1)
    def _():
        finish_slice(acc1_ref if (n_tiles - 1) % 2 else acc0_ref)


def _mlp(x, mod, l, g_pre, g_post, w_up, w_dn, w_in_next, w_out_next, *, tm, tf):
    b, s, d = x.shape
    f = w_up.shape[1]
    nj = f // tf
    rows = tm // nj
    tiles_per_seq = s // tm
    n_tiles = b * tiles_per_seq
    slices_per_seq = s // rows
    assert tiles_per_seq >= 2 and rows % NORM_ROWS == 0
    cast_next = w_in_next is not None

    def norm_tile(p):
        return jnp.minimum(p, n_tiles - 1)

    def done_tile(p):
        return jnp.maximum(p - 2, 0)

    def norm_rows(p, j):
        g = norm_tile(p) * nj + j
        return g // slices_per_seq, g % slices_per_seq, 0

    def done_rows(p, j):
        g = jnp.where(p < 2, 0, (p - 2) * nj + j)
        return g // slices_per_seq, g % slices_per_seq, 0

    def slab(p, j):
        return jnp.where(p == 0, 0, jnp.where(p == n_tiles + 1, nj - 1, j))

    def mod_spec(k, tile):
        return pl.BlockSpec((None, None, None, 1, d), lambda p, j: (l, tile(p) // tiles_per_seq, k, 0, 0))

    def g_spec():
        return pl.BlockSpec((None, 1, d), lambda p, j: (l, 0, 0))

    in_specs = [
        pl.BlockSpec((None, rows, d), norm_rows),
        pl.BlockSpec((None, rows, d), done_rows),
        mod_spec(3, norm_tile), mod_spec(4, norm_tile), mod_spec(5, done_tile), g_spec(), g_spec(),
        pl.BlockSpec((d, tf), lambda p, j: (0, slab(p, j))),
        pl.BlockSpec((tf, d), lambda p, j: (slab(p, j), 0)),
    ]
    out_specs = [pl.BlockSpec((None, rows, d), done_rows)]
    out_shape = [jax.ShapeDtypeStruct((b, s, d), F32)]
    args = [x, x, mod, mod, mod, g_pre, g_post, w_up, w_dn]
    n_out_blocks = 0
    if cast_next:
        n_steps = (n_tiles + 2) * nj
        n_out_blocks = 64
        out_rows = d // n_out_blocks
        assert n_steps >= N_PACK_BLOCKS and n_steps >= n_out_blocks

        def pack_blk(p, j):
            return jnp.minimum(p * nj + j, N_PACK_BLOCKS - 1)

        def out_blk(p, j):
            return jnp.minimum(p * nj + j, n_out_blocks - 1)

        in_specs += [
            pl.BlockSpec((pl.Element(1), pl.Element(PACK_COLS), pl.Element(d)),
                         lambda p, j: (l + 1, _pack_src_row(pack_blk(p, j)), 0)),
            pl.BlockSpec((None, out_rows, d), lambda p, j: (l + 1, out_blk(p, j), 0)),
        ]
        out_specs += [
            pl.BlockSpec((d, PACK_COLS), lambda p, j: (0, pack_blk(p, j))),
            pl.BlockSpec((out_rows, d), lambda p, j: (out_blk(p, j), 0)),
        ]
        out_shape += [jax.ShapeDtypeStruct((d, PK_END), BF16), jax.ShapeDtypeStruct((d, d), BF16)]
        args += [w_in_next, w_out_next]

    return pl.pallas_call(
        functools.partial(_mlp_kernel, n_tiles=n_tiles, rows=rows, cast_next=cast_next, n_out_blocks=n_out_blocks),
        grid=(n_tiles + 2, nj),
        in_specs=in_specs,
        out_specs=out_specs,
        out_shape=out_shape,
        scratch_shapes=[pltpu.VMEM((tm, d), BF16), pltpu.VMEM((tm, d), BF16),
                        pltpu.VMEM((tm, d), F32), pltpu.VMEM((tm, d), F32)],
        compiler_params=_params(2),
        name="mlp",
    )(*args)


def _cast_kernel(win_ref, wout_ref, winb_ref, woutb_ref, *, n_out_blocks):
    t = pl.program_id(0)
    _pack_w_in_block(t, win_ref, winb_ref)

    @pl.when(t < n_out_blocks)
    def _():
        woutb_ref[...] = wout_ref[...].astype(BF16)


def _cast_first_layer(w_in_t, w_out, *, rows):
    d = w_out.shape[1]
    n_out_blocks = d // rows
    assert n_out_blocks <= N_PACK_BLOCKS

    def out_blk(t):
        return jnp.minimum(t, n_out_blocks - 1)

    return pl.pallas_call(
        functools.partial(_cast_kernel, n_out_blocks=n_out_blocks),
        grid=(N_PACK_BLOCKS,),
        in_specs=[
            pl.BlockSpec((pl.Element(1), pl.Element(PACK_COLS), pl.Element(d)),
                         lambda t: (0, _pack_src_row(t), 0)),
            pl.BlockSpec((None, rows, d), lambda t: (0, out_blk(t), 0)),
        ],
        out_specs=[
            pl.BlockSpec((d, PACK_COLS), lambda t: (0, t)),
            pl.BlockSpec((rows, d), lambda t: (out_blk(t), 0)),
        ],
        out_shape=[jax.ShapeDtypeStruct((d, PK_END), BF16), jax.ShapeDtypeStruct((d, d), BF16)],
        compiler_params=_params(1),
        name="cast_first_layer",
    )(w_in_t, w_out)


def kernel(x, c, w_ada, b_ada, g_pre_mix, g_post_mix, g_pre_mlp, g_post_mlp, w_in, conv_w, conv_b,
           b_i, b_f, g_mlstm_head, g_attn_out, attn_sinks, w_out, w_up, w_down):
    b, s, d = x.shape
    depth = w_ada.shape[0]
    nh = MLSTM_HEADS
    assert d == D_MODEL and w_in.shape[2] == SRC_END and conv_w.shape[2] == 2 * MLSTM_QK_W

    c_pad = jnp.pad(c, ((0, 8 - b), (0, 0)))
    mod = _ada(c_pad, w_ada, b_ada).reshape(depth, 8, N_MOD, 1, d)

    def vec(a):
        return a.reshape(depth, 1, a.shape[-1])

    g_pre_mix, g_post_mix, g_pre_mlp, g_post_mlp = map(vec, (g_pre_mix, g_post_mix, g_pre_mlp, g_post_mlp))

    conv_b, g_attn_out = vec(conv_b), vec(g_attn_out)
    g_head = g_mlstm_head.reshape(depth, 1, MLSTM_V_W)
    gbias = jnp.pad(jnp.concatenate([b_i, b_f], axis=1), ((0, 0), (0, GATE_LANES - 2 * nh)))
    gbias = gbias.reshape(depth, 1, GATE_LANES)

    w_in_t = jnp.swapaxes(w_in, 1, 2)
    w_in_b, w_out_b = _cast_first_layer(w_in_t, w_out, rows=64)

    for l in range(depth):
        qk, v, og, att, gr, w_up_b, w_dn_b = _in_proj(
            x, mod, l, g_pre_mix, w_in_b, conv_w, conv_b, gbias, w_up, w_down, tm=256, chunk=MLSTM_CHUNK)
        hm = _mlstm(qk, v, og, gr, g_head, l, chunk=MLSTM_CHUNK, n_chunks=4)
        ha = _swa(att, attn_sinks, g_attn_out, l, n_sub=4)
        x = _out_proj(hm, ha, w_out_b, x, mod, l, g_post_mix, tm=512)
        if l + 1 < depth:
            x, w_in_b, w_out_b = _mlp(x, mod, l, g_pre_mlp, g_post_mlp, w_up_b, w_dn_b, w_in_t, w_out,
                                      tm=1024, tf=1024)
        else:
            x, = _mlp(x, mod, l, g_pre_mlp, g_post_mlp, w_up_b, w_dn_b, None, None, tm=1024, tf=1024)
    return x
```
